```python
import math
import jax, jax.numpy as jnp
from jax import lax
import numpy as np

D_MODEL = 1024
BATCH = 16
SEQ = 2048
DEPTH = 4

N_EVEN = (DEPTH + 1) // 2
N_ODD = DEPTH // 2
DN_ALPHA = (2.0 * DEPTH) ** 0.25
DN_BETA = (8.0 * DEPTH) ** -0.25
LN_EPS = 1e-5
RMS_EPS = 1e-6
Q_BLOCK = 128

A_HEADS = 8
A_HEAD_DIM = 64
A_KV_LATENT = 128
IDX_HEADS = 8
IDX_DIM = 64
TOPK_MAX = 256

B_HEADS = 4
B_DK = 64
B_DV = 128
B_GATE_RANK = 16
B_GATE_TEMP = 16.0
B_CHUNK = 64

EVEN_SPLITS = (A_HEADS * A_HEAD_DIM, A_KV_LATENT, IDX_HEADS * IDX_DIM, IDX_DIM, IDX_HEADS,
               B_HEADS * B_DK, B_HEADS * B_DK, B_HEADS * B_DV, B_GATE_RANK, B_HEADS * B_DV)
EVEN_IN = sum(EVEN_SPLITS)
EVEN_MIX = A_HEADS * A_HEAD_DIM + B_HEADS * B_DV

C_HEADS = 8
C_QK_DIM = 64
C_V_DIM = 2 * C_QK_DIM
C_MIX = C_HEADS * C_V_DIM

D_FF = 2816
N_EXPERTS = 8
TOP_K = 2
D_FF_EXPERT = 2816

kernel_name = 'hybrid_dsa_gla_diffattn_moe_deepnorm'


def layer_norm(x, g, b):
    xf = x.astype(jnp.float32)
    mu = jnp.mean(xf, axis=-1, keepdims=True)
    var = jnp.mean(jnp.square(xf - mu), axis=-1, keepdims=True)
    return ((xf - mu) * lax.rsqrt(var + LN_EPS) * g + b).astype(x.dtype)


def rms_norm(x, g):
    xf = x.astype(jnp.float32)
    return (xf * lax.rsqrt(jnp.mean(jnp.square(xf), axis=-1, keepdims=True) + RMS_EPS) * g).astype(x.dtype)


def split_cols(t, sizes):
    out, off = [], 0
    for s in sizes:
        out.append(t[..., off:off + s])
        off += s
    return out


def query_blocks(t):
    bsz, seq = t.shape[0], t.shape[1]
    return t.reshape((bsz, seq // Q_BLOCK, Q_BLOCK) + t.shape[2:]).swapaxes(0, 1)


def dsa_attention(q, c_kv, q_idx, k_idx, w_idx, kv_norm_g, w_uk, w_uv, kidx_ln_g, kidx_ln_b):
    bsz, seq = q.shape[0], q.shape[1]
    n_sel = min(TOPK_MAX, seq // 4)
    c = rms_norm(c_kv, kv_norm_g)
    k = c @ w_uk
    v = c @ w_uv
    k_i = layer_norm(k_idx, kidx_ln_g, kidx_ln_b).astype(jnp.float32)
    key_pos = jnp.arange(seq)
    gather = jax.vmap(lambda t, idx: t[idx])

    def block(args):
        qb, qib, wb, start = args
        qpos = start + jnp.arange(Q_BLOCK)
        rel = jax.nn.relu(jnp.einsum('bqhd,bsd->bqhs', qib.astype(jnp.float32), k_i) * IDX_DIM ** -0.5)
        score = jnp.einsum('bqh,bqhs->bqs', wb.astype(jnp.float32) * IDX_HEADS ** -0.5, rel)
        causal = key_pos[None, :] <= qpos[:, None]
        score = jnp.where(causal[None], score, -jnp.inf)
        _, idx = lax.top_k(score, n_sel)
        k_sel = gather(k, idx)
        v_sel = gather(v, idx)
        logits = jnp.einsum('bqhd,bqkd->bqhk', qb, k_sel).astype(jnp.float32) * A_HEAD_DIM ** -0.5
        valid = (idx <= qpos[None, :, None])[:, :, None, :]
        p = jax.nn.softmax(jnp.where(valid, logits, -jnp.inf), axis=-1).astype(v.dtype)
        return jnp.einsum('bqhk,bqkd->bqhd', p, v_sel)

    starts = jnp.arange(seq // Q_BLOCK) * Q_BLOCK
    out = lax.map(block, (query_blocks(q), query_blocks(q_idx), query_blocks(w_idx), starts))
    return out.swapaxes(0, 1).reshape(bsz, seq, A_HEADS * A_HEAD_DIM)


def gla_chunked(q, k, v, log_a):
    bsz, seq, nh, dk = q.shape
    dv = v.shape[-1]
    n = seq // B_CHUNK

    def chunks(t):
        return t.reshape(bsz, n, B_CHUNK, nh, t.shape[-1]).transpose(1, 0, 3, 2, 4).astype(jnp.float32)

    causal = jnp.tril(jnp.ones((B_CHUNK, B_CHUNK), dtype=bool))[None, None, :, :, None]

    def step(state, inp):
        qb, kb, vb, gb = inp
        b = jnp.cumsum(gb, axis=2)
        diff = b[:, :, :, None, :] - b[:, :, None, :, :]
        decay = jnp.exp(jnp.where(causal, diff, -jnp.inf))
        scores = jnp.einsum('bhid,bhjd,bhijd->bhij', qb, kb, decay)
        o = (jnp.einsum('bhij,bhje->bhie', scores, vb)
             + jnp.einsum('bhid,bhde->bhie', qb * jnp.exp(b), state))
        b_last = b[:, :, -1, :]
        state = (jnp.exp(b_last)[..., None] * state
                 + jnp.einsum('bhjd,bhje->bhde', kb * jnp.exp(b_last[:, :, None, :] - b), vb))
        return state, o

    state0 = jnp.zeros((bsz, nh, dk, dv), jnp.float32)
    _, o = lax.scan(step, state0, (chunks(q * dk ** -0.5), chunks(k), chunks(v), chunks(log_a)))
    return o.transpose(1, 0, 3, 2, 4).reshape(bsz, seq, nh, dv).astype(v.dtype)


def even_mixer(x, w_in, a_kv_norm_g, a_w_uk, a_w_uv, a_kidx_ln_g, a_kidx_ln_b, b_w_g2, b_b_g, b_norm_g, w_out):
    bsz, seq, _ = x.shape
    aq, ackv, aqi, aki, awi, bq, bk, bv, bg, br = split_cols(x @ w_in, EVEN_SPLITS)
    a_out = dsa_attention(aq.reshape(bsz, seq, A_HEADS, A_HEAD_DIM), ackv,
                          aqi.reshape(bsz, seq, IDX_HEADS, IDX_DIM), aki, awi,
                          a_kv_norm_g, a_w_uk, a_w_uv, a_kidx_ln_g, a_kidx_ln_b)
    log_a = jax.nn.log_sigmoid((bg @ b_w_g2 + b_b_g).astype(jnp.float32)) / B_GATE_TEMP
    heads = lambda t, d: t.reshape(bsz, seq, B_HEADS, d)
    o = gla_chunked(heads(bq, B_DK), heads(bk, B_DK), heads(bv, B_DV), heads(log_a, B_DK))
    b_out = (rms_norm(o, b_norm_g) * jax.nn.silu(heads(br, B_DV))).reshape(bsz, seq, B_HEADS * B_DV)
    return jnp.concatenate([a_out, b_out], axis=-1) @ w_out


def diff_attention(x, w_qkv, lam_q1, lam_k1, lam_q2, lam_k2, subln_g, w_out, lambda_init):
    bsz, seq, _ = x.shape
    f32 = jnp.float32
    q, k, v = split_cols(x @ w_qkv, (C_MIX, C_MIX, C_MIX))
    q = q.reshape(bsz, seq, C_HEADS, 2, C_QK_DIM)
    k = k.reshape(bsz, seq, C_HEADS, 2, C_QK_DIM)
    v = v.reshape(bsz, seq, C_HEADS, C_V_DIM)
    lam = (jnp.exp(jnp.sum(lam_q1.astype(f32) * lam_k1.astype(f32)))
           - jnp.exp(jnp.sum(lam_q2.astype(f32) * lam_k2.astype(f32))) + lambda_init)
    key_pos = jnp.arange(seq)

    def block(args):
        qb, start = args
        qpos = start + jnp.arange(Q_BLOCK)
        s = jnp.einsum('bqhmd,bshmd->bhmqs', qb, k).astype(f32) * C_QK_DIM ** -0.5
        s = jnp.where(key_pos[None, :] <= qpos[:, None], s, -jnp.inf)
        p = jax.nn.softmax(s, axis=-1)
        w = (p[:, :, 0] - lam * p[:, :, 1]).astype(v.dtype)
        return jnp.einsum('bhqs,bshe->bqhe', w, v)

    starts = jnp.arange(seq // Q_BLOCK) * Q_BLOCK
    o = lax.map(block, (query_blocks(q), starts)).swapaxes(0, 1).reshape(bsz, seq, C_HEADS, C_V_DIM)
    o = rms_norm(o, subln_g) * (1.0 - lambda_init)
    return o.reshape(bsz, seq, C_MIX) @ w_out


def swiglu(x, w_gate, w_up, w_down):
    return (jax.nn.silu(x @ w_gate) * (x @ w_up)) @ w_down


def moe_swiglu(x, w_router, b_router, w_gate, w_up, w_down):
    logits = (x @ w_router).astype(jnp.float32) + b_router
    top_vals, top_idx = lax.top_k(logits, TOP_K)
    gates = jax.nn.softmax(top_vals, axis=-1)
    combine = jnp.sum(jax.nn.one_hot(top_idx, N_EXPERTS, dtype=jnp.float32) * gates[..., None], axis=-2)
    y = jnp.zeros_like(x)
    for e in range(N_EXPERTS):
        y = y + combine[..., e:e + 1].astype(x.dtype) * swiglu(x, w_gate[e], w_up[e], w_down[e])
    return y


def setup_inputs(seed: int = 0) -> dict:
    key = jax.random.key(seed)
    keys = list(jax.random.split(key, 48))
    D = D_MODEL

    def nrm(shape, scale):
        return scale * jax.random.normal(keys.pop(), shape, jnp.float32)

    def gain(shape):
        return 1.0 + nrm(shape, 0.02)

    return {
        'x': nrm((BATCH, SEQ, D), 1.0),
        'ev_w_in': nrm((N_EVEN, D, EVEN_IN), D ** -0.5),
        'ev_a_kv_norm_g': gain((N_EVEN, A_KV_LATENT)),
        'ev_a_w_uk': nrm((N_EVEN, A_KV_LATENT, A_HEAD_DIM), A_KV_LATENT ** -0.5),
        'ev_a_w_uv': nrm((N_EVEN, A_KV_LATENT, A_HEAD_DIM), A_KV_LATENT ** -0.5),
        'ev_a_kidx_ln_g': gain((N_EVEN, IDX_DIM)),
        'ev_a_kidx_ln_b': nrm((N_EVEN, IDX_DIM), 0.02),
        'ev_b_w_g2': nrm((N_EVEN, B_GATE_RANK, B_HEADS * B_DK), B_GATE_RANK ** -0.5),
        'ev_b_b_g': nrm((N_EVEN, B_HEADS * B_DK), 0.1),
        'ev_b_norm_g': gain((N_EVEN, B_HEADS, B_DV)),
        'ev_w_out': nrm((N_EVEN, EVEN_MIX, D), DN_BETA * EVEN_MIX ** -0.5),
        'ev_ln1_g': gain((N_EVEN, D)),
        'ev_ln1_b': nrm((N_EVEN, D), 0.02),
        'ev_ffn_w_gate': nrm((N_EVEN, D, D_FF), D ** -0.5),
        'ev_ffn_w_up': nrm((N_EVEN, D, D_FF), D ** -0.5),
        'ev_ffn_w_down': nrm((N_EVEN, D_FF, D), DN_BETA * D_FF ** -0.5),
        'ev_ln2_g': gain((N_EVEN, D)),
        'ev_ln2_b': nrm((N_EVEN, D), 0.02),
        'od_w_qkv': nrm((N_ODD, D, 3 * C_MIX), D ** -0.5),
        'od_lam_q1': nrm((N_ODD, C_QK_DIM), 0.1),
        'od_lam_k1': nrm((N_ODD, C_QK_DIM), 0.1),
        'od_lam_q2': nrm((N_ODD, C_QK_DIM), 0.1),
        'od_lam_k2': nrm((N_ODD, C_QK_DIM), 0.1),
        'od_subln_g': gain((N_ODD, C_V_DIM)),
        'od_w_out': nrm((N_ODD, C_MIX, D), DN_BETA * C_MIX ** -0.5),
        'od_ln1_g': gain((N_ODD, D)),
        'od_ln1_b': nrm((N_ODD, D), 0.02),
        'od_router_w': nrm((N_ODD, D, N_EXPERTS), D ** -0.5),
        'od_router_b': nrm((N_ODD, N_EXPERTS), 0.01),
        'od_moe_w_gate': nrm((N_ODD, N_EXPERTS, D, D_FF_EXPERT), D ** -0.5),
        'od_moe_w_up': nrm((N_ODD, N_EXPERTS, D, D_FF_EXPERT), D ** -0.5),
        'od_moe_w_down': nrm((N_ODD, N_EXPERTS, D_FF_EXPERT, D), DN_BETA * D_FF_EXPERT ** -0.5),
        'od_ln2_g': gain((N_ODD, D)),
        'od_ln2_b': nrm((N_ODD, D), 0.02),
    }


def reference(x, ev_w_in, ev_a_kv_norm_g, ev_a_w_uk, ev_a_w_uv, ev_a_kidx_ln_g, ev_a_kidx_ln_b,
              ev_b_w_g2, ev_b_b_g, ev_b_norm_g, ev_w_out, ev_ln1_g, ev_ln1_b,
              ev_ffn_w_gate, ev_ffn_w_up, ev_ffn_w_down, ev_ln2_g, ev_ln2_b,
              od_w_qkv, od_lam_q1, od_lam_k1, od_lam_q2, od_lam_k2, od_subln_g, od_w_out,
              od_ln1_g, od_ln1_b, od_router_w, od_router_b,
              od_moe_w_gate, od_moe_w_up, od_moe_w_down, od_ln2_g, od_ln2_b):
    for i in range(DEPTH):
        j = i // 2
        if i % 2 == 0:
            h = even_mixer(x, ev_w_in[j], ev_a_kv_norm_g[j], ev_a_w_uk[j], ev_a_w_uv[j],
                           ev_a_kidx_ln_g[j], ev_a_kidx_ln_b[j], ev_b_w_g2[j], ev_b_b_g[j],
                           ev_b_norm_g[j], ev_w_out[j])
            x = layer_norm(DN_ALPHA * x + h, ev_ln1_g[j], ev_ln1_b[j])
            h = swiglu(x, ev_ffn_w_gate[j], ev_ffn_w_up[j], ev_ffn_w_down[j])
            x = layer_norm(DN_ALPHA * x + h, ev_ln2_g[j], ev_ln2_b[j])
        else:
            lambda_init = 0.8 - 0.6 * math.exp(-0.3 * i)
            h = diff_attention(x, od_w_qkv[j], od_lam_q1[j], od_lam_k1[j], od_lam_q2[j], od_lam_k2[j],
                               od_subln_g[j], od_w_out[j], lambda_init)
            x = layer_norm(DN_ALPHA * x + h, od_ln1_g[j], od_ln1_b[j])
            h = moe_swiglu(x, od_router_w[j], od_router_b[j], od_moe_w_gate[j], od_moe_w_up[j], od_moe_w_down[j])
            x = layer_norm(DN_ALPHA * x + h, od_ln2_g[j], od_ln2_b[j])
    return x
```

```python
import functools
import math

import jax
import jax.numpy as jnp
from jax import lax
from jax.experimental import pallas as pl
from jax.experimental.pallas import tpu as pltpu

F32 = jnp.float32
BF16 = jnp.bfloat16
I32 = jnp.int32

DEPTH = 4
DN_ALPHA = (2.0 * DEPTH) ** 0.25
LN_EPS = 1e-5
RMS_EPS = 1e-6

A_HEADS = 8
A_HEAD_DIM = 64
A_KV_LATENT = 128
IDX_HEADS = 8
IDX_DIM = 64
TOPK_MAX = 256

B_HEADS = 4
B_DK = 64
B_DV = 128
B_GATE_RANK = 16
B_GATE_TEMP = 16.0
B_CHUNK = 64

C_HEADS = 8
C_QK_DIM = 64
C_V_DIM = 128

N_EXPERTS = 8

LANES = 128
VMEM_LIMIT = 56 * 1024 * 1024

NEG_BIG = -1e30
INT_MIN = -(2 ** 31)

NT_DIMS = (((1,), (1,)), ((), ()))
TN_DIMS = (((0,), (0,)), ((), ()))


def _cparams(sem):
    return pltpu.CompilerParams(dimension_semantics=sem, vmem_limit_bytes=VMEM_LIMIT)


def _layer_norm(y, g, b):
    mu = jnp.mean(y, axis=-1, keepdims=True)
    d = y - mu
    var = jnp.mean(d * d, axis=-1, keepdims=True)
    return d * lax.rsqrt(var + LN_EPS) * g + b


def _rms_norm(y, g):
    return y * lax.rsqrt(jnp.mean(y * y, axis=-1, keepdims=True) + RMS_EPS) * g


def _sigmoid(z):
    return 1.0 / (1.0 + jnp.exp(-z))


def _full(shape):
    return pl.BlockSpec(shape, lambda *_: (0,) * len(shape))


def _even_in_kernel(x_ref, wbig_ref, wsm_ref, kvg_ref, wukv_ref, lng_ref, lnb_ref, wg2_ref, bg2_ref,
                    aq_ref, qi_ref, bq_ref, bk_ref, bv_ref, br_ref, k_ref, v_ref, ki_ref, wi_ref, la_ref):
    xb = x_ref[...].astype(BF16)

    def proj(lo, hi):
        return jnp.dot(xb, wbig_ref[:, lo:hi], preferred_element_type=F32)

    aq_ref[...] = (proj(0, 512) * A_HEAD_DIM ** -0.5).astype(BF16)
    qi_ref[...] = (proj(512, 1024) * IDX_DIM ** -0.5).astype(BF16)
    bq_ref[...] = (proj(1024, 1280) * B_DK ** -0.5).astype(BF16)
    bk_ref[...] = proj(1280, 1536).astype(BF16)
    bv_ref[...] = proj(1536, 2048).astype(BF16)
    br_ref[...] = proj(2048, 2560).astype(BF16)

    sm = jnp.dot(xb, wsm_ref[...], preferred_element_type=F32)
    ckv = sm[:, 0:128]
    aki = sm[:, 128:192]
    awi = sm[:, 192:200]
    bg = sm[:, 200:216]

    c = _rms_norm(ckv, kvg_ref[...])
    kv = jnp.dot(c.astype(BF16), wukv_ref[...], preferred_element_type=F32)
    k_ref[...] = kv[:, 0:64].astype(BF16)
    v_ref[...] = kv[:, 64:128].astype(BF16)
    ki_ref[...] = _layer_norm(aki, lng_ref[...], lnb_ref[...]).astype(BF16)
    wi_ref[...] = awi * IDX_HEADS ** -0.5
    z = jnp.dot(bg, wg2_ref[...], preferred_element_type=F32, precision=lax.Precision.HIGHEST) + bg2_ref[...]
    la_ref[...] = (jnp.minimum(z, 0.0) - jnp.log1p(jnp.exp(-jnp.abs(z)))) * (1.0 / B_GATE_TEMP)


def _even_in(x, w_in, kv_g, w_uk, w_uv, ln_g, ln_b, w_g2, b_g, tm=512):
    T, D = x.shape
    o = [0, 512, 640, 1152, 1216, 1224, 1480, 1736, 2248, 2264, 2776]
    col = lambda i: w_in[:, o[i]:o[i + 1]]
    wbig = jnp.concatenate([col(0), col(2), col(5), col(6), col(7), col(9)], axis=1).astype(BF16)
    wsm = jnp.concatenate([col(1), col(3), col(4), col(8), jnp.zeros((D, 40), F32)], axis=1).astype(BF16)
    wukv = jnp.concatenate([w_uk, w_uv], axis=1).astype(BF16)
    outs = [(512, BF16), (512, BF16), (256, BF16), (256, BF16), (512, BF16), (512, BF16),
            (64, BF16), (64, BF16), (64, BF16), (8, F32), (256, F32)]
    return pl.pallas_call(
        _even_in_kernel,
        grid=(T // tm,),
        in_specs=[pl.BlockSpec((tm, D), lambda i: (i, 0)),
                  _full((D, 2560)), _full((D, 256)), _full((1, 128)), _full((128, 128)),
                  _full((1, 64)), _full((1, 64)), _full((16, 256)), _full((1, 256))],
        out_specs=[pl.BlockSpec((tm, n), lambda i: (i, 0)) for n, _ in outs],
        out_shape=[jax.ShapeDtypeStruct((T, n), dt) for n, dt in outs],
        compiler_params=_cparams(("parallel",)),
        name="even_in",
    )(x, wbig, wsm, kv_g.reshape(1, 128), wukv, ln_g.reshape(1, 64), ln_b.reshape(1, 64),
      w_g2, b_g.reshape(1, 256))


DSA_TQ = 128
DSA_KC = 256


def _dsa_kernel(aq_ref, qi_ref, wi_ref, ki_ref, k_ref, v_ref, o_ref,
                keys_ref, wb_ref, midx_ref, m_ref, l_ref, acc_ref, *, seq, n_sel):
    TQ, KC = DSA_TQ, DSA_KC
    nb = KC // LANES
    qb = pl.program_id(1)
    n_kc = (qb * TQ + TQ + KC - 1) // KC
    qpos = qb * TQ + lax.broadcasted_iota(I32, (TQ, KC), 0)
    lane_pos = lax.broadcasted_iota(I32, (TQ, KC), 1)
    lane_blk = lax.broadcasted_iota(I32, (TQ, LANES), 1)

    w = wi_ref[...]
    for h in range(IDX_HEADS):
        wb_ref[h] = jnp.broadcast_to(w[:, h:h + 1], (TQ, LANES))

    def idx_body(c, carry):
        off = pl.multiple_of(c * KC, KC)
        kic = ki_ref[pl.ds(off, KC), :]
        score = jnp.zeros((TQ, KC), F32)
        for h in range(IDX_HEADS):
            qh = qi_ref[:, h * IDX_DIM:(h + 1) * IDX_DIM]
            rel = lax.dot_general(qh, kic, NT_DIMS, preferred_element_type=F32)
            wbh = wb_ref[h]
            score = score + jnp.concatenate([wbh] * nb, axis=1) * jnp.maximum(rel, 0.0)
        score = jnp.where(score == 0.0, 0.0, score)
        bits = lax.bitcast_convert_type(score, I32)
        key = jnp.where(bits < 0, bits ^ jnp.int32(0x7FFFFFFF), bits)
        keys_ref[c] = jnp.where(off + lane_pos <= qpos, key, jnp.int32(INT_MIN))
        return carry

    lax.fori_loop(0, n_kc, idx_body, 0)

    def count(pred):
        def body(c, acc):
            kc = keys_ref[c]
            off = c * KC
            for j in range(nb):
                blk = kc[:, j * LANES:(j + 1) * LANES]
                pos = off + j * LANES + lane_blk
                acc = acc + jnp.where(pred(blk, pos), 1.0, 0.0)
            return acc
        acc = lax.fori_loop(0, n_kc, body, jnp.zeros((TQ, LANES), F32))
        return jnp.broadcast_to(jnp.sum(acc, axis=1, keepdims=True), (TQ, LANES))

    def bit_body(i, t):
        cand = t + lax.shift_left(jnp.int32(1), jnp.int32(31) - i)
        cnt = count(lambda blk, pos: blk >= cand)
        return jnp.where(cnt >= float(n_sel), cand, t)

    t = lax.fori_loop(0, 32, bit_body, jnp.full((TQ, LANES), INT_MIN, I32))
    c_gt = count(lambda blk, pos: blk > t)
    c_ge = count(lambda blk, pos: blk >= t)
    need = float(n_sel) - c_gt
    has_row = t != jnp.int32(INT_MIN)
    midx_ref[...] = jnp.where(has_row, jnp.int32(seq), jnp.int32(-1))
    tie = jnp.where(has_row & (c_ge > float(n_sel)), 1.0, 0.0)

    @pl.when(jnp.max(tie) > 0.0)
    def _():
        def jbit(i, m):
            cand = m + lax.shift_left(jnp.int32(1), jnp.int32(11) - i)
            cnt = count(lambda blk, pos: (blk == t) & (pos < cand))
            return jnp.where(cnt < need, cand, m)
        m = lax.fori_loop(0, 12, jbit, jnp.zeros((TQ, LANES), I32))
        midx_ref[...] = jnp.where(has_row, m, jnp.int32(-1))

    midx = midx_ref[...]

    m_ref[...] = jnp.full(m_ref.shape, NEG_BIG, F32)
    l_ref[...] = jnp.zeros(l_ref.shape, F32)
    acc_ref[...] = jnp.zeros(acc_ref.shape, F32)

    def att_body(c, carry):
        off = pl.multiple_of(c * KC, KC)
        kc = keys_ref[c]
        tt = jnp.concatenate([t] * nb, axis=1)
        mm = jnp.concatenate([midx] * nb, axis=1)
        sel = (kc > tt) | ((kc == tt) & (off + lane_pos <= mm))
        bias = jnp.where(sel, 0.0, NEG_BIG)
        kk = k_ref[pl.ds(off, KC), :]
        vv = v_ref[pl.ds(off, KC), :]
        for h in range(A_HEADS):
            qh = aq_ref[:, h * A_HEAD_DIM:(h + 1) * A_HEAD_DIM]
            s = lax.dot_general(qh, kk, NT_DIMS, preferred_element_type=F32) + bias
            m_old = m_ref[h]
            m_new = jnp.maximum(m_old, jnp.max(s, axis=1, keepdims=True))
            alpha = jnp.exp(m_old - m_new)
            p = jnp.exp(s - jnp.concatenate([m_new] * nb, axis=1))
            l_ref[h] = alpha * l_ref[h] + jnp.sum(p, axis=1, keepdims=True)
            acc_ref[h] = alpha[:, :A_HEAD_DIM] * acc_ref[h] + jnp.dot(
                p.astype(BF16), vv, preferred_element_type=F32)
            m_ref[h] = m_new
        return carry

    lax.fori_loop(0, n_kc, att_body, 0)
    for h in range(A_HEADS):
        o_ref[:, h * A_HEAD_DIM:(h + 1) * A_HEAD_DIM] = (
            acc_ref[h] / l_ref[h][:, :A_HEAD_DIM]).astype(BF16)


def _dsa(aq, qi, wi, ki, k, v, batch, seq):
    T = aq.shape[0]
    TQ, KC = DSA_TQ, DSA_KC
    nq = seq // TQ
    n_sel = min(TOPK_MAX, seq // 4)
    rows = lambda n: pl.BlockSpec((TQ, n), lambda b, q: (b * nq + q, 0))
    whole = lambda n: pl.BlockSpec((seq, n), lambda b, q: (b, 0))
    return pl.pallas_call(
        functools.partial(_dsa_kernel, seq=seq, n_sel=n_sel),
        grid=(batch, nq),
        in_specs=[rows(512), rows(512), rows(8), whole(64), whole(64), whole(64)],
        out_specs=rows(512),
        out_shape=jax.ShapeDtypeStruct((T, 512), BF16),
        scratch_shapes=[pltpu.VMEM((seq // KC, TQ, KC), I32),
                        pltpu.VMEM((IDX_HEADS, TQ, LANES), F32),
                        pltpu.VMEM((TQ, LANES), I32),
                        pltpu.VMEM((A_HEADS, TQ, LANES), F32),
                        pltpu.VMEM((A_HEADS, TQ, LANES), F32),
                        pltpu.VMEM((A_HEADS, TQ, A_HEAD_DIM), F32)],
        compiler_params=_cparams(("parallel", "arbitrary")),
        name="dsa",
    )(aq, qi, wi, ki, k, v)


def _gla_kernel(q_ref, k_ref, v_ref, la_ref, r_ref, ng_ref, o_ref, st_ref, *, seq):
    C = B_CHUNK
    half = C // 2
    row = lax.broadcasted_iota(I32, (C, C), 0)
    colm = lax.broadcasted_iota(I32, (C, C), 1)
    tril = row >= colm
    trilf = jnp.where(tril, 1.0, 0.0)
    st_ref[...] = jnp.zeros(st_ref.shape, F32)

    def body(c, carry):
        off = pl.multiple_of(c * C, C)
        la = la_ref[pl.ds(off, C), :]
        b = jnp.dot(trilf, la, preferred_element_type=F32, precision=lax.Precision.HIGHEST)
        b_mid = b[half - 1:half, :]
        b_last = b[C - 1:C, :]
        q = q_ref[pl.ds(off, C), :].astype(F32)
        k = k_ref[pl.ds(off, C), :].astype(F32)
        q_in = (q * jnp.exp(b - b_mid)).astype(BF16)
        k_in = (k * jnp.exp(b_mid - b)).astype(BF16)
        q_st = (q * jnp.exp(b)).astype(BF16)
        k_st = (k * jnp.exp(b_last - b)).astype(BF16)
        e_last = jnp.exp(b_last)
        for h in range(B_HEADS):
            ks = slice(h * B_DK, (h + 1) * B_DK)
            vs = slice(h * B_DV, (h + 1) * B_DV)
            a = lax.dot_general(q_in[:, ks], k_in[:, ks], NT_DIMS, preferred_element_type=F32)
            a = jnp.where(tril, a, 0.0).astype(BF16)
            vh = v_ref[pl.ds(off, C), vs]
            st = st_ref[h]
            o = (jnp.dot(a, vh, preferred_element_type=F32)
                 + lax.dot_general(q_st[:, ks], st.astype(BF16), NT_DIMS, preferred_element_type=F32))
            st_ref[h] = st * e_last[:, ks] + lax.dot_general(vh, k_st[:, ks], TN_DIMS,
                                                            preferred_element_type=F32)
            r = r_ref[pl.ds(off, C), vs].astype(F32)
            o_ref[pl.ds(off, C), vs] = (_rms_norm(o, ng_ref[h:h + 1, :]) * (r * _sigmoid(r))).astype(BF16)
        return carry

    lax.fori_loop(0, seq // C, body, 0)


def _gla(bq, bk, bv, la, br, norm_g, batch, seq):
    T = bq.shape[0]
    blk = lambda n: pl.BlockSpec((seq, n), lambda b: (b, 0))
    return pl.pallas_call(
        functools.partial(_gla_kernel, seq=seq),
        grid=(batch,),
        in_specs=[blk(256), blk(256), blk(512), blk(256), blk(512), _full((B_HEADS, B_DV))],
        out_specs=blk(512),
        out_shape=jax.ShapeDtypeStruct((T, 512), BF16),
        scratch_shapes=[pltpu.VMEM((B_HEADS, B_DV, B_DK), F32)],
        compiler_params=_cparams(("parallel",)),
        name="gla",
    )(bq, bk, bv, la, br, norm_g)


def _proj_ln_kernel(*refs, n_in, router):
    acts = refs[:n_in]
    ws = refs[n_in:2 * n_in]
    x_ref, g_ref, b_ref = refs[2 * n_in:2 * n_in + 3]
    rest = refs[2 * n_in + 3:]
    h = jnp.dot(acts[0][...], ws[0][...], preferred_element_type=F32)
    for a, w in zip(acts[1:], ws[1:]):
        h = h + jnp.dot(a[...], w[...], preferred_element_type=F32)
    y = _layer_norm(DN_ALPHA * x_ref[...] + h, g_ref[...], b_ref[...])
    if not router:
        rest[0][...] = y
        return
    wr_ref, br_ref, o_ref, idx_ref, gate_ref = rest
    o_ref[...] = y
    logits = jnp.dot(y, wr_ref[...], preferred_element_type=F32, precision=lax.Precision.HIGHEST) + br_ref[...]
    tm = logits.shape[0]
    eid = lax.broadcasted_iota(I32, (tm, N_EXPERTS), 1)
    m1 = jnp.max(logits, axis=1, keepdims=True)
    i1 = jnp.min(jnp.where(logits == m1, eid, N_EXPERTS), axis=1, keepdims=True)
    rest_l = jnp.where(eid == i1, -jnp.inf, logits)
    m2 = jnp.max(rest_l, axis=1, keepdims=True)
    i2 = jnp.min(jnp.where(rest_l == m2, eid, N_EXPERTS), axis=1, keepdims=True)
    e2 = jnp.exp(m2 - m1)
    g1 = 1.0 / (1.0 + e2)
    g2 = e2 / (1.0 + e2)
    two = lax.broadcasted_iota(I32, (tm, 2), 1)
    idx_ref[...] = jnp.where(two == 0, i1, i2)
    gate_ref[...] = jnp.where(two == 0, g1, g2)


def _proj_ln(acts, ws, x, g, b, router=None, tm=512):
    T, D = x.shape
    n_in = len(acts)
    in_specs = ([pl.BlockSpec((tm, a.shape[1]), lambda i: (i, 0)) for a in acts]
                + [_full(w.shape) for w in ws]
                + [pl.BlockSpec((tm, D), lambda i: (i, 0)), _full((1, D)), _full((1, D))])
    args = list(acts) + [w.astype(BF16) for w in ws] + [x, g.reshape(1, D), b.reshape(1, D)]
    out_specs = [pl.BlockSpec((tm, D), lambda i: (i, 0))]
    out_shape = [jax.ShapeDtypeStruct((T, D), F32)]
    if router is not None:
        w_r, b_r = router
        in_specs += [_full((D, N_EXPERTS)), _full((1, N_EXPERTS))]
        args += [w_r, b_r.reshape(1, N_EXPERTS)]
        out_specs += [pl.BlockSpec((tm, 2), lambda i: (i, 0))] * 2
        out_shape += [jax.ShapeDtypeStruct((T, 2), I32), jax.ShapeDtypeStruct((T, 2), F32)]
    res = pl.pallas_call(
        functools.partial(_proj_ln_kernel, n_in=n_in, router=router is not None),
        grid=(T // tm,),
        in_specs=in_specs, out_specs=out_specs, out_shape=out_shape,
        compiler_params=_cparams(("parallel",)),
        name="proj_ln_router" if router is not None else "proj_ln",
    )(*args)
    return res if router is not None else res[0]


FFN_TF = 256


def _ffn_core(x_ref, wg_ref, wu_ref, wd_ref, xb_ref, acc_ref, f):
    @pl.when(f == 0)
    def _():
        xb_ref[...] = x_ref[...].astype(BF16)
        acc_ref[...] = jnp.zeros(acc_ref.shape, F32)

    xb = xb_ref[...]
    gt = jnp.dot(xb, wg_ref[...], preferred_element_type=F32)
    up = jnp.dot(xb, wu_ref[...], preferred_element_type=F32)
    hid = (gt * _sigmoid(gt) * up).astype(BF16)
    acc_ref[...] += jnp.dot(hid, wd_ref[...], preferred_element_type=F32)


def _ffn_ln_kernel(x_ref, wg_ref, wu_ref, wd_ref, g_ref, b_ref, o_ref, xb_ref, acc_ref):
    f = pl.program_id(1)
    _ffn_core(x_ref, wg_ref, wu_ref, wd_ref, xb_ref, acc_ref, f)

    @pl.when(f == pl.num_programs(1) - 1)
    def _():
        o_ref[...] = _layer_norm(DN_ALPHA * x_ref[...] + acc_ref[...], g_ref[...], b_ref[...])


def _ffn_ln(x, w_gate, w_up, w_down, g, b, tm=1024):
    T, D = x.shape
    F = w_gate.shape[1]
    tf = FFN_TF
    return pl.pallas_call(
        _ffn_ln_kernel,
        grid=(T // tm, F // tf),
        in_specs=[pl.BlockSpec((tm, D), lambda i, f: (i, 0)),
                  pl.BlockSpec((D, tf), lambda i, f: (0, f)),
                  pl.BlockSpec((D, tf), lambda i, f: (0, f)),
                  pl.BlockSpec((tf, D), lambda i, f: (f, 0)),
                  _full((1, D)), _full((1, D))],
        out_specs=pl.BlockSpec((tm, D), lambda i, f: (i, 0)),
        out_shape=jax.ShapeDtypeStruct((T, D), F32),
        scratch_shapes=[pltpu.VMEM((tm, D), BF16), pltpu.VMEM((tm, D), F32)],
        compiler_params=_cparams(("parallel", "arbitrary")),
        name="ffn_ln",
    )(x, w_gate.astype(BF16), w_up.astype(BF16), w_down.astype(BF16), g.reshape(1, D), b.reshape(1, D))


def _moe_ffn_kernel(te_ref, x_ref, wg_ref, wu_ref, wd_ref, o_ref, xb_ref, acc_ref):
    f = pl.program_id(1)
    _ffn_core(x_ref, wg_ref, wu_ref, wd_ref, xb_ref, acc_ref, f)

    @pl.when(f == pl.num_programs(1) - 1)
    def _():
        o_ref[...] = acc_ref[...]


def _moe_ffn(xs, tile_e, w_gate, w_up, w_down, tm):
    R, D = xs.shape
    F = w_gate.shape[2]
    tf = FFN_TF
    grid_spec = pltpu.PrefetchScalarGridSpec(
        num_scalar_prefetch=1,
        grid=(R // tm, F // tf),
        in_specs=[pl.BlockSpec((tm, D), lambda i, f, te: (i, 0)),
                  pl.BlockSpec((None, D, tf), lambda i, f, te: (te[i], 0, f)),
                  pl.BlockSpec((None, D, tf), lambda i, f, te: (te[i], 0, f)),
                  pl.BlockSpec((None, tf, D), lambda i, f, te: (te[i], f, 0))],
        out_specs=pl.BlockSpec((tm, D), lambda i, f, te: (i, 0)),
        scratch_shapes=[pltpu.VMEM((tm, D), BF16), pltpu.VMEM((tm, D), F32)],
    )
    return pl.pallas_call(
        _moe_ffn_kernel,
        grid_spec=grid_spec,
        out_shape=jax.ShapeDtypeStruct((R, D), F32),
        compiler_params=_cparams(("parallel", "arbitrary")),
        name="moe_ffn",
    )(tile_e, xs, w_gate.astype(BF16), w_up.astype(BF16), w_down.astype(BF16))


MOE_TR = 256


def _row_copy(src_ref, s_row, dst_ref, d_row, sem):
    return pltpu.make_async_copy(src_ref.at[pl.ds(s_row, 1), :], dst_ref.at[pl.ds(d_row, 1), :], sem)


def _moe_scatter_kernel(dest_ref, x_ref, init_ref, o_ref, sem):
    del init_ref
    def issue(r, carry):
        for s in range(2):
            _row_copy(x_ref, r, o_ref, dest_ref[0, 2 * r + s], sem).start()
        return carry
    lax.fori_loop(0, MOE_TR, issue, 0)

    def drain(r, carry):
        for s in range(2):
            _row_copy(x_ref, 0, o_ref, 0, sem).wait()
        return carry
    lax.fori_loop(0, MOE_TR, drain, 0)


def _moe_scatter(x, dest3, n_rows):
    T, D = x.shape
    nt = T // MOE_TR
    return pl.pallas_call(
        _moe_scatter_kernel,
        grid=(nt,),
        in_specs=[pl.BlockSpec((None, 1, 2 * MOE_TR), lambda i: (i, 0, 0), memory_space=pltpu.SMEM),
                  pl.BlockSpec((MOE_TR, D), lambda i: (i, 0)),
                  pl.BlockSpec(memory_space=pl.ANY)],
        out_specs=pl.BlockSpec(memory_space=pl.ANY),
        out_shape=jax.ShapeDtypeStruct((n_rows, D), F32),
        scratch_shapes=[pltpu.SemaphoreType.DMA(())],
        input_output_aliases={2: 0},
        compiler_params=_cparams(("arbitrary",)),
        name="moe_scatter",
    )(dest3, x, jnp.zeros((n_rows, D), F32))


def _moe_combine_kernel(dest_ref, ys_ref, gate_ref, x_ref, g_ref, b_ref, o_ref, buf0, buf1, sem):
    bufs = (buf0, buf1)
    def issue(r, carry):
        for s in range(2):
            _row_copy(ys_ref, dest_ref[0, 2 * r + s], bufs[s], r, sem).start()
        return carry
    lax.fori_loop(0, MOE_TR, issue, 0)

    def drain(r, carry):
        for s in range(2):
            _row_copy(ys_ref, 0, bufs[s], 0, sem).wait()
        return carry
    lax.fori_loop(0, MOE_TR, drain, 0)

    gate = gate_ref[...]
    y = gate[:, 0:1] * buf0[...] + gate[:, 1:2] * buf1[...]
    o_ref[...] = _layer_norm(DN_ALPHA * x_ref[...] + y, g_ref[...], b_ref[...])


def _moe_combine(ys, dest3, gate, x, g, b):
    T, D = x.shape
    nt = T // MOE_TR
    return pl.pallas_call(
        _moe_combine_kernel,
        grid=(nt,),
        in_specs=[pl.BlockSpec((None, 1, 2 * MOE_TR), lambda i: (i, 0, 0), memory_space=pltpu.SMEM),
                  pl.BlockSpec(memory_space=pl.ANY),
                  pl.BlockSpec((MOE_TR, 2), lambda i: (i, 0)),
                  pl.BlockSpec((MOE_TR, D), lambda i: (i, 0)),
                  _full((1, D)), _full((1, D))],
        out_specs=pl.BlockSpec((MOE_TR, D), lambda i: (i, 0)),
        out_shape=jax.ShapeDtypeStruct((T, D), F32),
        scratch_shapes=[pltpu.VMEM((MOE_TR, D), F32), pltpu.VMEM((MOE_TR, D), F32),
                        pltpu.SemaphoreType.DMA(())],
        compiler_params=_cparams(("arbitrary",)),
        name="moe_combine",
    )(dest3, ys, gate, x, g.reshape(1, D), b.reshape(1, D))


def _moe(x, idx, gate, w_gate, w_up, w_down, g, b, tm=512):
    T, D = x.shape
    pairs = 2 * T
    e = idx.reshape(pairs)
    onehot = (e[:, None] == jnp.arange(N_EXPERTS, dtype=I32)[None, :]).astype(I32)
    csum = jnp.cumsum(onehot, axis=0)
    rank = jnp.sum((csum - onehot) * onehot, axis=1)
    counts = csum[-1]
    padded = ((counts + tm - 1) // tm) * tm
    ends = jnp.cumsum(padded)
    starts = ends - padded
    dest = jnp.sum(onehot * starts[None, :], axis=1) + rank
    n_rows = pairs + N_EXPERTS * tm
    tiles = jnp.arange(n_rows // tm, dtype=I32) * tm
    tile_e = jnp.minimum(jnp.sum((tiles[:, None] >= ends[None, :]).astype(I32), axis=1), N_EXPERTS - 1)
    dest3 = dest.astype(I32).reshape(T // MOE_TR, 1, 2 * MOE_TR)
    xs = _moe_scatter(x, dest3, n_rows)
    ys = _moe_ffn(xs, tile_e.astype(I32), w_gate, w_up, w_down, tm)
    return _moe_combine(ys, dest3, gate, x, g, b)


DIFF_TQ = 256


def _qkv_kernel(x_ref, w_ref, q_ref, k_ref, v_ref):
    xb = x_ref[...].astype(BF16)
    n = q_ref.shape[1]
    q_ref[...] = (jnp.dot(xb, w_ref[:, 0:n], preferred_element_type=F32) * C_QK_DIM ** -0.5).astype(BF16)
    k_ref[...] = jnp.dot(xb, w_ref[:, n:2 * n], preferred_element_type=F32).astype(BF16)
    v_ref[...] = jnp.dot(xb, w_ref[:, 2 * n:3 * n], preferred_element_type=F32).astype(BF16)


def _qkv(x, w, tm=512):
    T, D = x.shape
    n = w.shape[1] // 3
    return pl.pallas_call(
        _qkv_kernel,
        grid=(T // tm,),
        in_specs=[pl.BlockSpec((tm, D), lambda i: (i, 0)), _full(w.shape)],
        out_specs=[pl.BlockSpec((tm, n), lambda i: (i, 0))] * 3,
        out_shape=[jax.ShapeDtypeStruct((T, n), BF16)] * 3,
        compiler_params=_cparams(("parallel",)),
        name="qkv",
    )(x, w.astype(BF16))


def _diff_kernel(lam_ref, q_ref, k_ref, v_ref, g_ref, o_ref, m_ref, l_ref, acc_ref, *, lambda_init):
    TQ = DIFF_TQ
    nb = TQ // LANES
    qi = pl.program_id(2)
    row = lax.broadcasted_iota(I32, (TQ, TQ), 0)
    colm = lax.broadcasted_iota(I32, (TQ, TQ), 1)
    m_ref[...] = jnp.full(m_ref.shape, NEG_BIG, F32)
    l_ref[...] = jnp.zeros(l_ref.shape, F32)
    acc_ref[...] = jnp.zeros(acc_ref.shape, F32)

    def step(c, diagonal):
        off = pl.multiple_of(c * TQ, TQ)
        kc = k_ref[pl.ds(off, TQ), :]
        vc = v_ref[pl.ds(off, TQ), :]
        for j in range(2):
            sl = slice(j * C_QK_DIM, (j + 1) * C_QK_DIM)
            s = lax.dot_general(q_ref[:, sl], kc[:, sl], NT_DIMS, preferred_element_type=F32)
            if diagonal:
                s = jnp.where(colm <= row, s, NEG_BIG)
            m_old = m_ref[j]
            m_new = jnp.maximum(m_old, jnp.max(s, axis=1, keepdims=True))
            alpha = jnp.exp(m_old - m_new)
            p = jnp.exp(s - jnp.concatenate([m_new] * nb, axis=1))
            l_ref[j] = alpha * l_ref[j] + jnp.sum(p, axis=1, keepdims=True)
            acc_ref[j] = alpha * acc_ref[j] + jnp.dot(p.astype(BF16), vc, preferred_element_type=F32)
            m_ref[j] = m_new

    def body(c, carry):
        step(c, False)
        return carry

    lax.fori_loop(0, qi, body, 0)
    step(qi, True)

    lam_v = lam_ref[...]
    lam = (jnp.exp(jnp.sum(lam_v[0:1] * lam_v[1:2], axis=1, keepdims=True))
           - jnp.exp(jnp.sum(lam_v[2:3] * lam_v[3:4], axis=1, keepdims=True)) + lambda_init)
    o = acc_ref[0] / l_ref[0] - lam * (acc_ref[1] / l_ref[1])
    o_ref[...] = (_rms_norm(o, g_ref[...]) * (1.0 - lambda_init)).astype(BF16)


def _diff_attn(q, k, v, lam4, subln_g, lambda_init, batch, seq):
    T = q.shape[0]
    TQ = DIFF_TQ
    nq = seq // TQ
    return pl.pallas_call(
        functools.partial(_diff_kernel, lambda_init=lambda_init),
        grid=(batch, C_HEADS, nq),
        in_specs=[_full((4, C_QK_DIM)),
                  pl.BlockSpec((TQ, C_V_DIM), lambda b, h, i: (b * nq + i, h)),
                  pl.BlockSpec((seq, C_V_DIM), lambda b, h, i: (b, h)),
                  pl.BlockSpec((seq, C_V_DIM), lambda b, h, i: (b, h)),
                  _full((1, C_V_DIM))],
        out_specs=pl.BlockSpec((TQ, C_V_DIM), lambda b, h, i: (b * nq + i, h)),
        out_shape=jax.ShapeDtypeStruct((T, C_HEADS * C_V_DIM), BF16),
        scratch_shapes=[pltpu.VMEM((2, TQ, LANES), F32), pltpu.VMEM((2, TQ, LANES), F32),
                        pltpu.VMEM((2, TQ, C_V_DIM), F32)],
        compiler_params=_cparams(("parallel", "parallel", "arbitrary")),
        name="diff_attn",
    )(lam4, q, k, v, subln_g.reshape(1, C_V_DIM))


def _even_layer(x, p, j, batch, seq):
    (aq, qi, bq, bk, bv, br, k, v, ki, wi, la) = _even_in(
        x, p['ev_w_in'][j], p['ev_a_kv_norm_g'][j], p['ev_a_w_uk'][j], p['ev_a_w_uv'][j],
        p['ev_a_kidx_ln_g'][j], p['ev_a_kidx_ln_b'][j], p['ev_b_w_g2'][j], p['ev_b_b_g'][j])
    a_out = _dsa(aq, qi, wi, ki, k, v, batch, seq)
    b_out = _gla(bq, bk, bv, la, br, p['ev_b_norm_g'][j], batch, seq)
    w_out = p['ev_w_out'][j]
    na = A_HEADS * A_HEAD_DIM
    x = _proj_ln([a_out, b_out], [w_out[:na], w_out[na:]], x, p['ev_ln1_g'][j], p['ev_ln1_b'][j])
    return _ffn_ln(x, p['ev_ffn_w_gate'][j], p['ev_ffn_w_up'][j], p['ev_ffn_w_down'][j],
                   p['ev_ln2_g'][j], p['ev_ln2_b'][j])


def _odd_layer(x, p, j, layer, batch, seq):
    lambda_init = 0.8 - 0.6 * math.exp(-0.3 * layer)
    q, k, v = _qkv(x, p['od_w_qkv'][j])
    lam4 = jnp.stack([p['od_lam_q1'][j], p['od_lam_k1'][j], p['od_lam_q2'][j], p['od_lam_k2'][j]])
    o = _diff_attn(q, k, v, lam4, p['od_subln_g'][j], lambda_init, batch, seq)
    x, idx, gate = _proj_ln([o], [p['od_w_out'][j]], x, p['od_ln1_g'][j], p['od_ln1_b'][j],
                            router=(p['od_router_w'][j], p['od_router_b'][j]))
    return _moe(x, idx, gate, p['od_moe_w_gate'][j], p['od_moe_w_up'][j], p['od_moe_w_down'][j],
                p['od_ln2_g'][j], p['od_ln2_b'][j])


def kernel(x, ev_w_in, ev_a_kv_norm_g, ev_a_w_uk, ev_a_w_uv, ev_a_kidx_ln_g, ev_a_kidx_ln_b,
           ev_b_w_g2, ev_b_b_g, ev_b_norm_g, ev_w_out, ev_ln1_g, ev_ln1_b,
           ev_ffn_w_gate, ev_ffn_w_up, ev_ffn_w_down, ev_ln2_g, ev_ln2_b,
           od_w_qkv, od_lam_q1, od_lam_k1, od_lam_q2, od_lam_k2, od_subln_g, od_w_out,
           od_ln1_g, od_ln1_b, od_router_w, od_router_b,
           od_moe_w_gate, od_moe_w_up, od_moe_w_down, od_ln2_g, od_ln2_b):
    p = dict(locals())
    batch, seq, d = x.shape
    h = x.reshape(batch * seq, d)
    for layer in range(DEPTH):
        j = layer // 2
        if layer % 2 == 0:
            h = _even_layer(h, p, j, batch, seq)
        else:
            h = _odd_layer(h, p, j, layer, batch, seq)
    return h.reshape(batch, seq, d)
```

```python
import functools
import math

import jax
import jax.numpy as jnp
from jax import lax
from jax.experimental import pallas as pl
from jax.experimental.pallas import tpu as pltpu

F32 = jnp.float32
BF16 = jnp.bfloat16
I32 = jnp.int32

DEPTH = 4
DN_ALPHA = (2.0 * DEPTH) ** 0.25
LN_EPS = 1e-5
RMS_EPS = 1e-6

A_HEADS = 8
A_HEAD_DIM = 64
A_KV_LATENT = 128
IDX_HEADS = 8
IDX_DIM = 64
TOPK_MAX = 256

B_HEADS = 4
B_DK = 64
B_DV = 128
B_GATE_RANK = 16
B_GATE_TEMP = 16.0
B_CHUNK = 64

C_HEADS = 8
C_QK_DIM = 64
C_V_DIM = 128

N_EXPERTS = 8

LANES = 128
SUBLANES = 8
VMEM_LIMIT = 56 * 1024 * 1024

NEG_BIG = -1e30
INT_MIN = -(2 ** 31)

NT_DIMS = (((1,), (1,)), ((), ()))
TN_DIMS = (((0,), (0,)), ((), ()))


def _cparams(sem):
    return pltpu.CompilerParams(dimension_semantics=sem, vmem_limit_bytes=VMEM_LIMIT)


def _layer_norm(y, g, b):
    mu = jnp.mean(y, axis=-1, keepdims=True)
    d = y - mu
    var = jnp.mean(d * d, axis=-1, keepdims=True)
    return d * lax.rsqrt(var + LN_EPS) * g + b


def _rms_norm(y, g):
    return y * lax.rsqrt(jnp.mean(y * y, axis=-1, keepdims=True) + RMS_EPS) * g


def _sigmoid(z):
    return 1.0 / (1.0 + jnp.exp(-z))


def _full(shape):
    return pl.BlockSpec(shape, lambda *_: (0,) * len(shape))


LOG2E = math.log2(math.e)
ONES_ROWS = 16


def _softmax_tile(s_t, m_old):
    m_new = jnp.maximum(m_old, jnp.max(s_t, axis=0, keepdims=True))
    alpha = jnp.exp2(m_old - m_new)
    p_t = jnp.exp2((s_t - m_new).astype(BF16))
    return m_new, alpha, p_t


def _with_ones_rows(vt):
    return jnp.concatenate([vt, jnp.ones((ONES_ROWS, vt.shape[1]), BF16)], axis=0)


def _even_in_kernel(x_ref, wbig_ref, wsm_ref, wwit_ref, kvg_ref, wuk_ref, wuvt_ref, lng_ref, lnb_ref,
                    wg2_ref, bg2_ref,
                    aq_ref, qi_ref, bq_ref, bk_ref, bv_ref, br_ref, k_ref, vt_ref, ki_ref, wit_ref, la_ref):
    xb = x_ref[...].astype(BF16)

    def proj(lo, hi):
        return jnp.dot(xb, wbig_ref[:, lo:hi], preferred_element_type=F32)

    aq = (proj(0, 512) * (A_HEAD_DIM ** -0.5 * LOG2E)).astype(BF16)
    qi = (proj(512, 1024) * IDX_DIM ** -0.5).astype(BF16)
    for h in range(A_HEADS):
        aq_ref[h] = aq[:, h * A_HEAD_DIM:(h + 1) * A_HEAD_DIM]
        qi_ref[h] = qi[:, h * IDX_DIM:(h + 1) * IDX_DIM]
    bq_ref[...] = (proj(1024, 1280) * B_DK ** -0.5).astype(BF16)
    bk_ref[...] = proj(1280, 1536).astype(BF16)
    bv_ref[...] = proj(1536, 2048).astype(BF16)
    br_ref[...] = proj(2048, 2560).astype(BF16)

    sm = jnp.dot(xb, wsm_ref[...], preferred_element_type=F32)
    ckv = sm[:, 0:128]
    aki = sm[:, 128:192]
    bg = sm[:, 192:208]

    c = _rms_norm(ckv, kvg_ref[...]).astype(BF16)
    k_ref[...] = jnp.dot(c, wuk_ref[...], preferred_element_type=F32).astype(BF16)
    vt_ref[...] = lax.dot_general(wuvt_ref[...], c, NT_DIMS, preferred_element_type=F32).astype(BF16)
    ki_ref[...] = _layer_norm(aki, lng_ref[...], lnb_ref[...]).astype(BF16)
    wit_ref[...] = lax.dot_general(wwit_ref[...], xb, NT_DIMS, preferred_element_type=F32) * IDX_HEADS ** -0.5
    z = jnp.dot(bg, wg2_ref[...], preferred_element_type=F32, precision=lax.Precision.HIGHEST) + bg2_ref[...]
    la_ref[...] = (jnp.minimum(z, 0.0) - jnp.log1p(jnp.exp(-jnp.abs(z)))) * (1.0 / B_GATE_TEMP)


def _even_in(x, w_in, kv_g, w_uk, w_uv, ln_g, ln_b, w_g2, b_g, tm=512):
    T, D = x.shape
    o = [0, 512, 640, 1152, 1216, 1224, 1480, 1736, 2248, 2264, 2776]
    col = lambda i: w_in[:, o[i]:o[i + 1]]
    wbig = jnp.concatenate([col(0), col(2), col(5), col(6), col(7), col(9)], axis=1).astype(BF16)
    wsm = jnp.concatenate([col(1), col(3), col(8), jnp.zeros((D, 48), F32)], axis=1).astype(BF16)
    wwit = col(4).T.astype(BF16)
    rows = lambda n, dt: (pl.BlockSpec((tm, n), lambda i: (i, 0)), jax.ShapeDtypeStruct((T, n), dt))
    cols = lambda n, dt: (pl.BlockSpec((n, tm), lambda i: (0, i)), jax.ShapeDtypeStruct((n, T), dt))
    heads = (pl.BlockSpec((A_HEADS, tm, A_HEAD_DIM), lambda i: (0, i, 0)),
             jax.ShapeDtypeStruct((A_HEADS, T, A_HEAD_DIM), BF16))
    outs = [heads, heads, rows(256, BF16), rows(256, BF16), rows(512, BF16), rows(512, BF16),
            rows(64, BF16), cols(64, BF16), rows(64, BF16), cols(8, F32), rows(256, F32)]
    return pl.pallas_call(
        _even_in_kernel,
        grid=(T // tm,),
        in_specs=[pl.BlockSpec((tm, D), lambda i: (i, 0)),
                  _full((D, 2560)), _full((D, 256)), _full((8, D)), _full((1, 128)), _full((128, 64)),
                  _full((64, 128)), _full((1, 64)), _full((1, 64)), _full((16, 256)), _full((1, 256))],
        out_specs=[s for s, _ in outs],
        out_shape=[s for _, s in outs],
        compiler_params=_cparams(("parallel",)),
        name="even_in",
    )(x, wbig, wsm, wwit, kv_g.reshape(1, 128), w_uk.astype(BF16), w_uv.T.astype(BF16),
      ln_g.reshape(1, 64), ln_b.reshape(1, 64), w_g2, b_g.reshape(1, 256))


DSA_TQ = 256
DSA_KC = 256


def _dsa_kernel(aq_ref, qi_ref, wit_ref, ki_ref, k_ref, vt_ref, o_ref,
                keys_ref, midx_ref, acc_ref, out_ref, *, seq, n_sel):
    TQ, KC = DSA_TQ, DSA_KC
    NQ = A_HEADS * TQ
    q_all = aq_ref[...].reshape(NQ, A_HEAD_DIM)
    qi_all = qi_ref[...].reshape(IDX_HEADS * TQ, IDX_DIM)
    qb = pl.program_id(1)
    n_kc = (qb * TQ + TQ + KC - 1) // KC
    kpos0 = lax.broadcasted_iota(I32, (KC, TQ), 0)
    qpos = qb * TQ + lax.broadcasted_iota(I32, (KC, TQ), 1)
    w_t = wit_ref[...]

    def idx_body(c, carry):
        off = pl.multiple_of(c * KC, KC)
        kic = ki_ref[pl.ds(off, KC), :]
        rel = lax.dot_general(kic, qi_all, NT_DIMS, preferred_element_type=F32)
        score = jnp.zeros((KC, TQ), F32)
        for h in range(IDX_HEADS):
            score = score + w_t[h:h + 1, :] * jnp.maximum(rel[:, h * TQ:(h + 1) * TQ], 0.0)
        score = jnp.where(score == 0.0, 0.0, score)
        bits = lax.bitcast_convert_type(score, I32)
        key = jnp.where(bits < 0, bits ^ jnp.int32(0x7FFFFFFF), bits)
        keys_ref[c] = jnp.where(off + kpos0 <= qpos, key, jnp.int32(INT_MIN))
        return carry

    lax.fori_loop(0, n_kc, idx_body, 0)

    def count(pred):
        def body(c, acc):
            hit = jnp.where(pred(keys_ref[c], c * KC + kpos0), 1.0, 0.0)
            return acc + jnp.sum(hit.reshape(KC // SUBLANES, SUBLANES, TQ), axis=0)
        acc = lax.fori_loop(0, n_kc, body, jnp.zeros((SUBLANES, TQ), F32))
        return jnp.sum(acc, axis=0, keepdims=True)

    def bit_body(i, t):
        cand = t + lax.shift_left(jnp.int32(1), jnp.int32(31) - i)
        cnt = count(lambda kc, pos: kc >= cand)
        return jnp.where(cnt >= float(n_sel), cand, t)

    t = lax.fori_loop(0, 32, bit_body, jnp.full((1, TQ), INT_MIN, I32))
    c_gt = count(lambda kc, pos: kc > t)
    c_ge = count(lambda kc, pos: kc >= t)
    need = float(n_sel) - c_gt
    has_row = t != jnp.int32(INT_MIN)
    midx_ref[...] = jnp.broadcast_to(jnp.where(has_row, jnp.int32(seq), jnp.int32(-1)), midx_ref.shape)
    tie = jnp.where(has_row & (c_ge > float(n_sel)), 1.0, 0.0)

    @pl.when(jnp.max(tie) > 0.0)
    def _():
        def jbit(i, m):
            cand = m + lax.shift_left(jnp.int32(1), jnp.int32(11) - i)
            cnt = count(lambda kc, pos: (kc == t) & (pos < cand))
            return jnp.where(cnt < need, cand, m)
        m = lax.fori_loop(0, 12, jbit, jnp.zeros((1, TQ), I32))
        midx_ref[...] = jnp.broadcast_to(jnp.where(has_row, m, jnp.int32(-1)), midx_ref.shape)

    midx = midx_ref[0:1, :]

    acc_ref[...] = jnp.zeros(acc_ref.shape, F32)

    def att_body(c, stats):
        m_old, l_old = stats
        off = pl.multiple_of(c * KC, KC)
        kc = keys_ref[c]
        sel = (kc > t) | ((kc == t) & (off + kpos0 <= midx))
        bias = jnp.where(sel, 0.0, NEG_BIG)
        kk = k_ref[pl.ds(off, KC), :]
        vt_aug = _with_ones_rows(vt_ref[:, pl.ds(off, KC)])
        s_t = lax.dot_general(kk, q_all, NT_DIMS, preferred_element_type=F32)
        s_t = s_t + jnp.concatenate([bias] * A_HEADS, axis=1)
        m_new, alpha, p_t = _softmax_tile(s_t, m_old)
        pv = jnp.dot(vt_aug, p_t, preferred_element_type=F32)
        acc_ref[...] = alpha * acc_ref[...] + pv[:A_HEAD_DIM]
        return m_new, alpha * l_old + pv[A_HEAD_DIM:A_HEAD_DIM + 1]

    _, l_fin = lax.fori_loop(0, n_kc, att_body,
                             (jnp.full((1, NQ), NEG_BIG, F32), jnp.zeros((1, NQ), F32)))
    o_t = acc_ref[...] / l_fin
    for h in range(A_HEADS):
        out_ref[h * A_HEAD_DIM:(h + 1) * A_HEAD_DIM, :] = o_t[:, h * TQ:(h + 1) * TQ]
    o_ref[...] = out_ref[...].T.astype(BF16)


def _dsa(aq, qi, wit, ki, k, vt, batch, seq):
    T = aq.shape[1]
    TQ, KC = DSA_TQ, DSA_KC
    nq = seq // TQ
    n_sel = min(TOPK_MAX, seq // 4)
    heads = pl.BlockSpec((A_HEADS, TQ, A_HEAD_DIM), lambda b, q: (0, b * nq + q, 0))
    whole = lambda n: pl.BlockSpec((seq, n), lambda b, q: (b, 0))
    return pl.pallas_call(
        functools.partial(_dsa_kernel, seq=seq, n_sel=n_sel),
        grid=(batch, nq),
        in_specs=[heads, heads,
                  pl.BlockSpec((IDX_HEADS, TQ), lambda b, q: (0, b * nq + q)),
                  whole(64), whole(64),
                  pl.BlockSpec((A_HEAD_DIM, seq), lambda b, q: (0, b))],
        out_specs=pl.BlockSpec((TQ, A_HEADS * A_HEAD_DIM), lambda b, q: (b * nq + q, 0)),
        out_shape=jax.ShapeDtypeStruct((T, A_HEADS * A_HEAD_DIM), BF16),
        scratch_shapes=[pltpu.VMEM((seq // KC, KC, TQ), I32),
                        pltpu.VMEM((SUBLANES, TQ), I32),
                        pltpu.VMEM((A_HEAD_DIM, A_HEADS * TQ), F32),
                        pltpu.VMEM((A_HEADS * A_HEAD_DIM, TQ), F32)],
        compiler_params=_cparams(("parallel", "arbitrary")),
        name="dsa",
    )(aq, qi, wit, ki, k, vt)


def _gla_kernel(q_ref, k_ref, v_ref, la_ref, r_ref, ng_ref, o_ref, st_ref, *, seq):
    C = B_CHUNK
    half = C // 2
    row = lax.broadcasted_iota(I32, (C, C), 0)
    colm = lax.broadcasted_iota(I32, (C, C), 1)
    tril = row >= colm
    trilf = jnp.where(tril, 1.0, 0.0)
    st_ref[...] = jnp.zeros(st_ref.shape, F32)

    def body(c, carry):
        off = pl.multiple_of(c * C, C)
        la = la_ref[pl.ds(off, C), :]
        b = jnp.dot(trilf, la, preferred_element_type=F32, precision=lax.Precision.HIGHEST)
        b_mid = b[half - 1:half, :]
        b_last = b[C - 1:C, :]
        q = q_ref[pl.ds(off, C), :].astype(F32)
        k = k_ref[pl.ds(off, C), :].astype(F32)
        q_in = (q * jnp.exp(b - b_mid)).astype(BF16)
        k_in = (k * jnp.exp(b_mid - b)).astype(BF16)
        q_st = (q * jnp.exp(b)).astype(BF16)
        k_st = (k * jnp.exp(b_last - b)).astype(BF16)
        e_last = jnp.exp(b_last)
        for h in range(B_HEADS):
            ks = slice(h * B_DK, (h + 1) * B_DK)
            vs = slice(h * B_DV, (h + 1) * B_DV)
            a = lax.dot_general(q_in[:, ks], k_in[:, ks], NT_DIMS, preferred_element_type=F32)
            a = jnp.where(tril, a, 0.0).astype(BF16)
            vh = v_ref[pl.ds(off, C), vs]
            st = st_ref[h]
            o = (jnp.dot(a, vh, preferred_element_type=F32)
                 + lax.dot_general(q_st[:, ks], st.astype(BF16), NT_DIMS, preferred_element_type=F32))
            st_ref[h] = st * e_last[:, ks] + lax.dot_general(vh, k_st[:, ks], TN_DIMS,
                                                            preferred_element_type=F32)
            r = r_ref[pl.ds(off, C), vs].astype(F32)
            o_ref[pl.ds(off, C), vs] = (_rms_norm(o, ng_ref[h:h + 1, :]) * (r * _sigmoid(r))).astype(BF16)
        return carry

    lax.fori_loop(0, seq // C, body, 0)


def _gla(bq, bk, bv, la, br, norm_g, batch, seq):
    T = bq.shape[0]
    blk = lambda n: pl.BlockSpec((seq, n), lambda b: (b, 0))
    return pl.pallas_call(
        functools.partial(_gla_kernel, seq=seq),
        grid=(batch,),
        in_specs=[blk(256), blk(256), blk(512), blk(256), blk(512), _full((B_HEADS, B_DV))],
        out_specs=blk(512),
        out_shape=jax.ShapeDtypeStruct((T, 512), BF16),
        scratch_shapes=[pltpu.VMEM((B_HEADS, B_DV, B_DK), F32)],
        compiler_params=_cparams(("parallel",)),
        name="gla",
    )(bq, bk, bv, la, br, norm_g)


def _proj_ln_kernel(*refs, n_in, head_major, router):
    acts = refs[:n_in]
    ws = refs[n_in:2 * n_in]
    x_ref, g_ref, b_ref = refs[2 * n_in:2 * n_in + 3]
    rest = refs[2 * n_in + 3:]
    if head_major:
        a_ref, w_ref = acts[0], ws[0]
        hd = a_ref.shape[2]
        h = jnp.dot(a_ref[0], w_ref[0:hd, :], preferred_element_type=F32)
        for i in range(1, a_ref.shape[0]):
            h = h + jnp.dot(a_ref[i], w_ref[i * hd:(i + 1) * hd, :], preferred_element_type=F32)
    else:
        h = jnp.dot(acts[0][...], ws[0][...], preferred_element_type=F32)
        for a, w in zip(acts[1:], ws[1:]):
            h = h + jnp.dot(a[...], w[...], preferred_element_type=F32)
    y = _layer_norm(DN_ALPHA * x_ref[...] + h, g_ref[...], b_ref[...])
    if not router:
        rest[0][...] = y
        return
    wr_ref, br_ref, o_ref, idx_ref, gate_ref = rest
    o_ref[...] = y
    logits = jnp.dot(y, wr_ref[...], preferred_element_type=F32, precision=lax.Precision.HIGHEST) + br_ref[...]
    tm = logits.shape[0]
    eid = lax.broadcasted_iota(I32, (tm, N_EXPERTS), 1)
    m1 = jnp.max(logits, axis=1, keepdims=True)
    i1 = jnp.min(jnp.where(logits == m1, eid, N_EXPERTS), axis=1, keepdims=True)
    rest_l = jnp.where(eid == i1, -jnp.inf, logits)
    m2 = jnp.max(rest_l, axis=1, keepdims=True)
    i2 = jnp.min(jnp.where(rest_l == m2, eid, N_EXPERTS), axis=1, keepdims=True)
    e2 = jnp.exp(m2 - m1)
    g1 = 1.0 / (1.0 + e2)
    g2 = e2 / (1.0 + e2)
    two = lax.broadcasted_iota(I32, (tm, 2), 1)
    idx_ref[...] = jnp.where(two == 0, i1, i2)
    gate_ref[...] = jnp.where(two == 0, g1, g2)


def _proj_ln(acts, ws, x, g, b, router=None, head_major=False, tm=512):
    T, D = x.shape
    n_in = len(acts)
    if head_major:
        nh, _, hd = acts[0].shape
        act_specs = [pl.BlockSpec((nh, tm, hd), lambda i: (0, i, 0))]
    else:
        act_specs = [pl.BlockSpec((tm, a.shape[1]), lambda i: (i, 0)) for a in acts]
    in_specs = (act_specs + [_full(w.shape) for w in ws]
                + [pl.BlockSpec((tm, D), lambda i: (i, 0)), _full((1, D)), _full((1, D))])
    args = list(acts) + [w.astype(BF16) for w in ws] + [x, g.reshape(1, D), b.reshape(1, D)]
    out_specs = [pl.BlockSpec((tm, D), lambda i: (i, 0))]
    out_shape = [jax.ShapeDtypeStruct((T, D), F32)]
    if router is not None:
        w_r, b_r = router
        in_specs += [_full((D, N_EXPERTS)), _full((1, N_EXPERTS))]
        args += [w_r, b_r.reshape(1, N_EXPERTS)]
        out_specs += [pl.BlockSpec((tm, 2), lambda i: (i, 0))] * 2
        out_shape += [jax.ShapeDtypeStruct((T, 2), I32), jax.ShapeDtypeStruct((T, 2), F32)]
    res = pl.pallas_call(
        functools.partial(_proj_ln_kernel, n_in=n_in, head_major=head_major, router=router is not None),
        grid=(T // tm,),
        in_specs=in_specs, out_specs=out_specs, out_shape=out_shape,
        compiler_params=_cparams(("parallel",)),
        name="proj_ln_router" if router is not None else "proj_ln",
    )(*args)
    return res if router is not None else res[0]


FFN_TF = 256


def _ffn_core(x_ref, wg_ref, wu_ref, wd_ref, xb_ref, acc_ref, f):
    @pl.when(f == 0)
    def _():
        xb_ref[...] = x_ref[...].astype(BF16)
        acc_ref[...] = jnp.zeros(acc_ref.shape, F32)

    xb = xb_ref[...]
    gt = jnp.dot(xb, wg_ref[...], preferred_element_type=F32)
    up = jnp.dot(xb, wu_ref[...], preferred_element_type=F32)
    hid = (gt * _sigmoid(gt) * up).astype(BF16)
    acc_ref[...] += jnp.dot(hid, wd_ref[...], preferred_element_type=F32)


def _ffn_ln_kernel(x_ref, wg_ref, wu_ref, wd_ref, g_ref, b_ref, o_ref, xb_ref, acc_ref):
    f = pl.program_id(1)
    _ffn_core(x_ref, wg_ref, wu_ref, wd_ref, xb_ref, acc_ref, f)

    @pl.when(f == pl.num_programs(1) - 1)
    def _():
        o_ref[...] = _layer_norm(DN_ALPHA * x_ref[...] + acc_ref[...], g_ref[...], b_ref[...])


def _ffn_ln(x, w_gate, w_up, w_down, g, b, tm=1024):
    T, D = x.shape
    F = w_gate.shape[1]
    tf = FFN_TF
    return pl.pallas_call(
        _ffn_ln_kernel,
        grid=(T // tm, F // tf),
        in_specs=[pl.BlockSpec((tm, D), lambda i, f: (i, 0)),
                  pl.BlockSpec((D, tf), lambda i, f: (0, f)),
                  pl.BlockSpec((D, tf), lambda i, f: (0, f)),
                  pl.BlockSpec((tf, D), lambda i, f: (f, 0)),
                  _full((1, D)), _full((1, D))],
        out_specs=pl.BlockSpec((tm, D), lambda i, f: (i, 0)),
        out_shape=jax.ShapeDtypeStruct((T, D), F32),
        scratch_shapes=[pltpu.VMEM((tm, D), BF16), pltpu.VMEM((tm, D), F32)],
        compiler_params=_cparams(("parallel", "arbitrary")),
        name="ffn_ln",
    )(x, w_gate.astype(BF16), w_up.astype(BF16), w_down.astype(BF16), g.reshape(1, D), b.reshape(1, D))


def _moe_ffn_kernel(te_ref, x_ref, wg_ref, wu_ref, wd_ref, o_ref, xb_ref, acc_ref):
    f = pl.program_id(1)
    _ffn_core(x_ref, wg_ref, wu_ref, wd_ref, xb_ref, acc_ref, f)

    @pl.when(f == pl.num_programs(1) - 1)
    def _():
        o_ref[...] = acc_ref[...]


def _moe_ffn(xs, tile_e, w_gate, w_up, w_down, tm):
    R, D = xs.shape
    F = w_gate.shape[2]
    tf = FFN_TF
    grid_spec = pltpu.PrefetchScalarGridSpec(
        num_scalar_prefetch=1,
        grid=(R // tm, F // tf),
        in_specs=[pl.BlockSpec((tm, D), lambda i, f, te: (i, 0)),
                  pl.BlockSpec((None, D, tf), lambda i, f, te: (te[i], 0, f)),
                  pl.BlockSpec((None, D, tf), lambda i, f, te: (te[i], 0, f)),
                  pl.BlockSpec((None, tf, D), lambda i, f, te: (te[i], f, 0))],
        out_specs=pl.BlockSpec((tm, D), lambda i, f, te: (i, 0)),
        scratch_shapes=[pltpu.VMEM((tm, D), BF16), pltpu.VMEM((tm, D), F32)],
    )
    return pl.pallas_call(
        _moe_ffn_kernel,
        grid_spec=grid_spec,
        out_shape=jax.ShapeDtypeStruct((R, D), F32),
        compiler_params=_cparams(("parallel", "arbitrary")),
        name="moe_ffn",
    )(tile_e, xs, w_gate.astype(BF16), w_up.astype(BF16), w_down.astype(BF16))


MOE_TR = 256


def _row_copy(src_ref, s_row, dst_ref, d_row, sem):
    return pltpu.make_async_copy(src_ref.at[pl.ds(s_row, 1), :], dst_ref.at[pl.ds(d_row, 1), :], sem)


def _moe_scatter_kernel(dest_ref, x_ref, init_ref, o_ref, sem):
    del init_ref
    def issue(r, carry):
        for s in range(2):
            _row_copy(x_ref, r, o_ref, dest_ref[0, 2 * r + s], sem).start()
        return carry
    lax.fori_loop(0, MOE_TR, issue, 0, unroll=8)

    def drain(r, carry):
        for s in range(2):
            _row_copy(x_ref, 0, o_ref, 0, sem).wait()
        return carry
    lax.fori_loop(0, MOE_TR, drain, 0, unroll=8)


def _moe_scatter(x, dest3, n_rows):
    T, D = x.shape
    nt = T // MOE_TR
    return pl.pallas_call(
        _moe_scatter_kernel,
        grid=(nt,),
        in_specs=[pl.BlockSpec((None, 1, 2 * MOE_TR), lambda i: (i, 0, 0), memory_space=pltpu.SMEM),
                  pl.BlockSpec((MOE_TR, D), lambda i: (i, 0)),
                  pl.BlockSpec(memory_space=pl.ANY)],
        out_specs=pl.BlockSpec(memory_space=pl.ANY),
        out_shape=jax.ShapeDtypeStruct((n_rows, D), F32),
        scratch_shapes=[pltpu.SemaphoreType.DMA(())],
        input_output_aliases={2: 0},
        compiler_params=_cparams(("arbitrary",)),
        name="moe_scatter",
    )(dest3, x, jnp.zeros((n_rows, D), F32))


def _moe_combine_kernel(dest_ref, ys_ref, gate_ref, x_ref, g_ref, b_ref, o_ref, buf0, buf1, sem):
    bufs = (buf0, buf1)
    def issue(r, carry):
        for s in range(2):
            _row_copy(ys_ref, dest_ref[0, 2 * r + s], bufs[s], r, sem).start()
        return carry
    lax.fori_loop(0, MOE_TR, issue, 0, unroll=8)

    def drain(r, carry):
        for s in range(2):
            _row_copy(ys_ref, 0, bufs[s], 0, sem).wait()
        return carry
    lax.fori_loop(0, MOE_TR, drain, 0, unroll=8)

    gate = gate_ref[...]
    y = gate[:, 0:1] * buf0[...] + gate[:, 1:2] * buf1[...]
    o_ref[...] = _layer_norm(DN_ALPHA * x_ref[...] + y, g_ref[...], b_ref[...])


def _moe_combine(ys, dest3, gate, x, g, b):
    T, D = x.shape
    nt = T // MOE_TR
    return pl.pallas_call(
        _moe_combine_kernel,
        grid=(nt,),
        in_specs=[pl.BlockSpec((None, 1, 2 * MOE_TR), lambda i: (i, 0, 0), memory_space=pltpu.SMEM),
                  pl.BlockSpec(memory_space=pl.ANY),
                  pl.BlockSpec((MOE_TR, 2), lambda i: (i, 0)),
                  pl.BlockSpec((MOE_TR, D), lambda i: (i, 0)),
                  _full((1, D)), _full((1, D))],
        out_specs=pl.BlockSpec((MOE_TR, D), lambda i: (i, 0)),
        out_shape=jax.ShapeDtypeStruct((T, D), F32),
        scratch_shapes=[pltpu.VMEM((MOE_TR, D), F32), pltpu.VMEM((MOE_TR, D), F32),
                        pltpu.SemaphoreType.DMA(())],
        compiler_params=_cparams(("arbitrary",)),
        name="moe_combine",
    )(dest3, ys, gate, x, g.reshape(1, D), b.reshape(1, D))


def _moe(x, idx, gate, w_gate, w_up, w_down, g, b, tm=1024):
    T, D = x.shape
    pairs = 2 * T
    e = idx.reshape(pairs)
    onehot = (e[:, None] == jnp.arange(N_EXPERTS, dtype=I32)[None, :]).astype(I32)
    csum = jnp.cumsum(onehot, axis=0)
    rank = jnp.sum((csum - onehot) * onehot, axis=1)
    counts = csum[-1]
    padded = ((counts + tm - 1) // tm) * tm
    ends = jnp.cumsum(padded)
    starts = ends - padded
    dest = jnp.sum(onehot * starts[None, :], axis=1) + rank
    n_rows = pairs + N_EXPERTS * tm
    tiles = jnp.arange(n_rows // tm, dtype=I32) * tm
    tile_e = jnp.minimum(jnp.sum((tiles[:, None] >= ends[None, :]).astype(I32), axis=1), N_EXPERTS - 1)
    dest3 = dest.astype(I32).reshape(T // MOE_TR, 1, 2 * MOE_TR)
    xs = _moe_scatter(x, dest3, n_rows)
    ys = _moe_ffn(xs, tile_e.astype(I32), w_gate, w_up, w_down, tm)
    return _moe_combine(ys, dest3, gate, x, g, b)


DIFF_TQ = 512
DIFF_HP = 2


def _qkv_kernel(x_ref, w_ref, q_ref, k_ref, vt_ref):
    xb = x_ref[...].astype(BF16)
    hd = q_ref.shape[2]
    n = C_HEADS * hd
    for h in range(C_HEADS):
        cs = slice(h * hd, (h + 1) * hd)
        q_ref[h] = (jnp.dot(xb, w_ref[:, cs], preferred_element_type=F32)
                    * (C_QK_DIM ** -0.5 * LOG2E)).astype(BF16)
        k_ref[h] = jnp.dot(xb, w_ref[:, n + h * hd:n + (h + 1) * hd], preferred_element_type=F32).astype(BF16)
        v = jnp.dot(xb, w_ref[:, 2 * n + h * hd:2 * n + (h + 1) * hd], preferred_element_type=F32)
        vt_ref[0, h] = v.astype(BF16).T


def _qkv(x, w, batch, seq, tm=512):
    T, D = x.shape
    hd = w.shape[1] // 3 // C_HEADS
    per = seq // tm
    hm = pl.BlockSpec((C_HEADS, tm, hd), lambda i: (0, i, 0))
    return pl.pallas_call(
        _qkv_kernel,
        grid=(T // tm,),
        in_specs=[pl.BlockSpec((tm, D), lambda i: (i, 0)), _full(w.shape)],
        out_specs=[hm, hm, pl.BlockSpec((1, C_HEADS, hd, tm), lambda i: (i // per, 0, 0, i % per))],
        out_shape=[jax.ShapeDtypeStruct((C_HEADS, T, hd), BF16)] * 2
                  + [jax.ShapeDtypeStruct((batch, C_HEADS, hd, seq), BF16)],
        compiler_params=_cparams(("parallel",)),
        name="qkv",
    )(x, w.astype(BF16))


def _diff_kernel(lam_ref, q_ref, k_ref, vt_ref, g_ref, o_ref, acc_ref, *, lambda_init):
    TQ = DIFF_TQ
    qi = pl.program_id(1)
    kidx = lax.broadcasted_iota(I32, (TQ, TQ), 0)
    qidx = lax.broadcasted_iota(I32, (TQ, TQ), 1)
    lam_v = lam_ref[...]
    lam = (jnp.exp(jnp.sum(lam_v[0:1] * lam_v[1:2], axis=1, keepdims=True))
           - jnp.exp(jnp.sum(lam_v[2:3] * lam_v[3:4], axis=1, keepdims=True)) + lambda_init)

    def heads_body(hg, carry):
        heads = [hg * DIFF_HP + i for i in range(DIFF_HP)]
        qs = [q_ref[h] for h in heads]

        def step(c, stats, diagonal):
            off = pl.multiple_of(c * TQ, TQ)
            out = []
            for i, h in enumerate(heads):
                kc = k_ref[h, pl.ds(off, TQ), :]
                vt_aug = _with_ones_rows(vt_ref[0, h, :, pl.ds(off, TQ)])
                for j in range(2):
                    n = 2 * i + j
                    sl = slice(j * C_QK_DIM, (j + 1) * C_QK_DIM)
                    s_t = lax.dot_general(kc[:, sl], qs[i][:, sl], NT_DIMS, preferred_element_type=F32)
                    if diagonal:
                        s_t = jnp.where(kidx <= qidx, s_t, NEG_BIG)
                    m_new, alpha, p_t = _softmax_tile(s_t, stats[2 * n])
                    pv = jnp.dot(vt_aug, p_t, preferred_element_type=F32)
                    acc_ref[n] = alpha * acc_ref[n] + pv[:C_V_DIM]
                    out += [m_new, alpha * stats[2 * n + 1] + pv[C_V_DIM:C_V_DIM + 1]]
            return tuple(out)

        acc_ref[...] = jnp.zeros(acc_ref.shape, F32)
        neg = jnp.full((1, TQ), NEG_BIG, F32)
        zero = jnp.zeros((1, TQ), F32)
        stats = lax.fori_loop(0, qi, lambda c, st: step(c, st, False), (neg, zero) * (2 * DIFF_HP))
        stats = step(qi, stats, True)
        for i, h in enumerate(heads):
            l0, l1 = stats[4 * i + 1], stats[4 * i + 3]
            o_t = acc_ref[2 * i] / l0 - lam * (acc_ref[2 * i + 1] / l1)
            o_t = o_t * lax.rsqrt(jnp.mean(o_t * o_t, axis=0, keepdims=True) + RMS_EPS)
            o_ref[h] = (o_t.T * g_ref[...] * (1.0 - lambda_init)).astype(BF16)
        return carry

    lax.fori_loop(0, C_HEADS // DIFF_HP, heads_body, 0)


def _diff_attn(q, k, vt, lam4, subln_g, lambda_init, batch, seq):
    nh, T, hd = q.shape
    TQ = DIFF_TQ
    nq = seq // TQ
    return pl.pallas_call(
        functools.partial(_diff_kernel, lambda_init=lambda_init),
        grid=(batch, nq),
        in_specs=[_full((4, C_QK_DIM)),
                  pl.BlockSpec((nh, TQ, hd), lambda b, i: (0, b * nq + i, 0)),
                  pl.BlockSpec((nh, seq, hd), lambda b, i: (0, b, 0)),
                  pl.BlockSpec((1, nh, hd, seq), lambda b, i: (b, 0, 0, 0)),
                  _full((1, C_V_DIM))],
        out_specs=pl.BlockSpec((nh, TQ, C_V_DIM), lambda b, i: (0, b * nq + i, 0)),
        out_shape=jax.ShapeDtypeStruct((nh, T, C_V_DIM), BF16),
        scratch_shapes=[pltpu.VMEM((2 * DIFF_HP, C_V_DIM, TQ), F32)],
        compiler_params=_cparams(("parallel", "arbitrary")),
        name="diff_attn",
    )(lam4, q, k, vt, subln_g.reshape(1, C_V_DIM))


def _even_layer(x, p, j, batch, seq):
    (aq, qi, bq, bk, bv, br, k, vt, ki, wit, la) = _even_in(
        x, p['ev_w_in'][j], p['ev_a_kv_norm_g'][j], p['ev_a_w_uk'][j], p['ev_a_w_uv'][j],
        p['ev_a_kidx_ln_g'][j], p['ev_a_kidx_ln_b'][j], p['ev_b_w_g2'][j], p['ev_b_b_g'][j])
    a_out = _dsa(aq, qi, wit, ki, k, vt, batch, seq)
    b_out = _gla(bq, bk, bv, la, br, p['ev_b_norm_g'][j], batch, seq)
    w_out = p['ev_w_out'][j]
    na = A_HEADS * A_HEAD_DIM
    x = _proj_ln([a_out, b_out], [w_out[:na], w_out[na:]], x, p['ev_ln1_g'][j], p['ev_ln1_b'][j])
    return _ffn_ln(x, p['ev_ffn_w_gate'][j], p['ev_ffn_w_up'][j], p['ev_ffn_w_down'][j],
                   p['ev_ln2_g'][j], p['ev_ln2_b'][j])


def _odd_layer(x, p, j, layer, batch, seq):
    lambda_init = 0.8 - 0.6 * math.exp(-0.3 * layer)
    q, k, vt = _qkv(x, p['od_w_qkv'][j], batch, seq)
    lam4 = jnp.stack([p['od_lam_q1'][j], p['od_lam_k1'][j], p['od_lam_q2'][j], p['od_lam_k2'][j]])
    o = _diff_attn(q, k, vt, lam4, p['od_subln_g'][j], lambda_init, batch, seq)
    x, idx, gate = _proj_ln([o], [p['od_w_out'][j]], x, p['od_ln1_g'][j], p['od_ln1_b'][j],
                            router=(p['od_router_w'][j], p['od_router_b'][j]), head_major=True)
    return _moe(x, idx, gate, p['od_moe_w_gate'][j], p['od_moe_w_up'][j], p['od_moe_w_down'][j],
                p['od_ln2_g'][j], p['od_ln2_b'][j])


def kernel(x, ev_w_in, ev_a_kv_norm_g, ev_a_w_uk, ev_a_w_uv, ev_a_kidx_ln_g, ev_a_kidx_ln_b,
           ev_b_w_g2, ev_b_b_g, ev_b_norm_g, ev_w_out, ev_ln1_g, ev_ln1_b,
           ev_ffn_w_gate, ev_ffn_w_up, ev_ffn_w_down, ev_ln2_g, ev_ln2_b,
           od_w_qkv, od_lam_q1, od_lam_k1, od_lam_q2, od_lam_k2, od_subln_g, od_w_out,
           od_ln1_g, od_ln1_b, od_router_w, od_router_b,
           od_moe_w_gate, od_moe_w_up, od_moe_w_down, od_ln2_g, od_ln2_b):
    p = dict(locals())
    batch, seq, d = x.shape
    h = x.reshape(batch * seq, d)
    for layer in range(DEPTH):
        j = layer // 2
        if layer % 2 == 0:
            h = _even_layer(h, p, j, batch, seq)
        else:
            h = _odd_layer(h, p, j, layer, batch, seq)
    return h.reshape(batch, seq, d)
```

```python
import functools
import math

import jax
import jax.numpy as jnp
from jax import lax
from jax.experimental import pallas as pl
from jax.experimental.pallas import tpu as pltpu

F32 = jnp.float32
BF16 = jnp.bfloat16
I32 = jnp.int32

DEPTH = 4
DN_ALPHA = (2.0 * DEPTH) ** 0.25
LN_EPS = 1e-5
RMS_EPS = 1e-6

A_HEADS = 8
A_HEAD_DIM = 64
A_KV_LATENT = 128
IDX_HEADS = 8
IDX_DIM = 64
TOPK_MAX = 256

B_HEADS = 4
B_DK = 64
B_DV = 128
B_GATE_RANK = 16
B_GATE_TEMP = 16.0
B_CHUNK = 64

C_HEADS = 8
C_QK_DIM = 64
C_V_DIM = 128

N_EXPERTS = 8

LANES = 128
SUBLANES = 8
VMEM_LIMIT = 56 * 1024 * 1024

NEG_BIG = -1e30
INT_MIN = -(2 ** 31)

NT_DIMS = (((1,), (1,)), ((), ()))
TN_DIMS = (((0,), (0,)), ((), ()))


def _cparams(sem):
    return pltpu.CompilerParams(dimension_semantics=sem, vmem_limit_bytes=VMEM_LIMIT)


def _layer_norm(y, g, b):
    mu = jnp.mean(y, axis=-1, keepdims=True)
    d = y - mu
    var = jnp.mean(d * d, axis=-1, keepdims=True)
    return d * lax.rsqrt(var + LN_EPS) * g + b


def _rms_norm(y, g):
    return y * lax.rsqrt(jnp.mean(y * y, axis=-1, keepdims=True) + RMS_EPS) * g


def _sigmoid(z):
    return 1.0 / (1.0 + jnp.exp(-z))


def _full(shape):
    return pl.BlockSpec(shape, lambda *_: (0,) * len(shape))


LOG2E = math.log2(math.e)
ONES_ROWS = 16


def _softmax_tile(s_t, m_old):
    m_new = jnp.maximum(m_old, jnp.max(s_t, axis=0, keepdims=True))
    alpha = jnp.exp2(m_old - m_new)
    p_t = jnp.exp2((s_t - m_new).astype(BF16))
    return m_new, alpha, p_t


def _with_ones_rows(vt):
    return jnp.concatenate([vt, jnp.ones((ONES_ROWS, vt.shape[1]), BF16)], axis=0)


def _even_in_kernel(x_ref, wbig_ref, wsm_ref, wwit_ref, kvg_ref, wuk_ref, wuvt_ref, lng_ref, lnb_ref,
                    wg2_ref, bg2_ref,
                    aq_ref, qi_ref, bq_ref, bk_ref, bv_ref, br_ref, k_ref, vt_ref, ki_ref, wit_ref, la_ref):
    xb = x_ref[...].astype(BF16)

    def proj(lo, hi):
        return jnp.dot(xb, wbig_ref[:, lo:hi], preferred_element_type=F32)

    aq = (proj(0, 512) * (A_HEAD_DIM ** -0.5 * LOG2E)).astype(BF16)
    qi = (proj(512, 1024) * IDX_DIM ** -0.5).astype(BF16)
    for h in range(A_HEADS):
        aq_ref[h] = aq[:, h * A_HEAD_DIM:(h + 1) * A_HEAD_DIM]
        qi_ref[h] = qi[:, h * IDX_DIM:(h + 1) * IDX_DIM]
    bq_ref[...] = (proj(1024, 1280) * B_DK ** -0.5).astype(BF16)
    bk_ref[...] = proj(1280, 1536).astype(BF16)
    bv_ref[...] = proj(1536, 2048).astype(BF16)
    br_ref[...] = proj(2048, 2560).astype(BF16)

    sm = jnp.dot(xb, wsm_ref[...], preferred_element_type=F32)
    ckv = sm[:, 0:128]
    aki = sm[:, 128:192]
    bg = sm[:, 192:208]

    c = _rms_norm(ckv, kvg_ref[...]).astype(BF16)
    k_ref[...] = jnp.dot(c, wuk_ref[...], preferred_element_type=F32).astype(BF16)
    vt_ref[...] = lax.dot_general(wuvt_ref[...], c, NT_DIMS, preferred_element_type=F32).astype(BF16)
    ki_ref[...] = _layer_norm(aki, lng_ref[...], lnb_ref[...]).astype(BF16)
    wit_ref[...] = lax.dot_general(wwit_ref[...], xb, NT_DIMS, preferred_element_type=F32) * IDX_HEADS ** -0.5
    z = jnp.dot(bg, wg2_ref[...], preferred_element_type=F32, precision=lax.Precision.HIGHEST) + bg2_ref[...]
    la_ref[...] = (jnp.minimum(z, 0.0) - jnp.log1p(jnp.exp(-jnp.abs(z)))) * (1.0 / B_GATE_TEMP)


def _even_in(x, w_in, kv_g, w_uk, w_uv, ln_g, ln_b, w_g2, b_g, tm=512):
    T, D = x.shape
    o = [0, 512, 640, 1152, 1216, 1224, 1480, 1736, 2248, 2264, 2776]
    col = lambda i: w_in[:, o[i]:o[i + 1]]
    wbig = jnp.concatenate([col(0), col(2), col(5), col(6), col(7), col(9)], axis=1).astype(BF16)
    wsm = jnp.concatenate([col(1), col(3), col(8), jnp.zeros((D, 48), F32)], axis=1).astype(BF16)
    wwit = col(4).T.astype(BF16)
    rows = lambda n, dt: (pl.BlockSpec((tm, n), lambda i: (i, 0)), jax.ShapeDtypeStruct((T, n), dt))
    cols = lambda n, dt: (pl.BlockSpec((n, tm), lambda i: (0, i)), jax.ShapeDtypeStruct((n, T), dt))
    heads = (pl.BlockSpec((A_HEADS, tm, A_HEAD_DIM), lambda i: (0, i, 0)),
             jax.ShapeDtypeStruct((A_HEADS, T, A_HEAD_DIM), BF16))
    outs = [heads, heads, rows(256, BF16), rows(256, BF16), rows(512, BF16), rows(512, BF16),
            rows(64, BF16), cols(64, BF16), rows(64, BF16), cols(8, F32), rows(256, F32)]
    return pl.pallas_call(
        _even_in_kernel,
        grid=(T // tm,),
        in_specs=[pl.BlockSpec((tm, D), lambda i: (i, 0)),
                  _full((D, 2560)), _full((D, 256)), _full((8, D)), _full((1, 128)), _full((128, 64)),
                  _full((64, 128)), _full((1, 64)), _full((1, 64)), _full((16, 256)), _full((1, 256))],
        out_specs=[s for s, _ in outs],
        out_shape=[s for _, s in outs],
        compiler_params=_cparams(("parallel",)),
        name="even_in",
    )(x, wbig, wsm, wwit, kv_g.reshape(1, 128), w_uk.astype(BF16), w_uv.T.astype(BF16),
      ln_g.reshape(1, 64), ln_b.reshape(1, 64), w_g2, b_g.reshape(1, 256))


DSA_TQ = 256
DSA_KC = 256


def _dsa_kernel(aq_ref, qi_ref, wit_ref, ki_ref, k_ref, vt_ref, o_ref,
                keys_ref, midx_ref, acc_ref, out_ref, *, seq, n_sel):
    TQ, KC = DSA_TQ, DSA_KC
    NQ = A_HEADS * TQ
    q_all = aq_ref[...].reshape(NQ, A_HEAD_DIM)
    qi_all = qi_ref[...].reshape(IDX_HEADS * TQ, IDX_DIM)
    qb = pl.program_id(1)
    n_kc = (qb * TQ + TQ + KC - 1) // KC
    kpos0 = lax.broadcasted_iota(I32, (KC, TQ), 0)
    qpos = qb * TQ + lax.broadcasted_iota(I32, (KC, TQ), 1)
    w_t = wit_ref[...]

    def idx_body(c, carry):
        off = pl.multiple_of(c * KC, KC)
        kic = ki_ref[pl.ds(off, KC), :]
        rel = lax.dot_general(kic, qi_all, NT_DIMS, preferred_element_type=F32)
        score = jnp.zeros((KC, TQ), F32)
        for h in range(IDX_HEADS):
            score = score + w_t[h:h + 1, :] * jnp.maximum(rel[:, h * TQ:(h + 1) * TQ], 0.0)
        score = jnp.where(score == 0.0, 0.0, score)
        bits = lax.bitcast_convert_type(score, I32)
        key = jnp.where(bits < 0, bits ^ jnp.int32(0x7FFFFFFF), bits)
        keys_ref[c] = jnp.where(off + kpos0 <= qpos, key, jnp.int32(INT_MIN))
        return carry

    lax.fori_loop(0, n_kc, idx_body, 0)

    def count(pred):
        def body(c, acc):
            hit = jnp.where(pred(keys_ref[c], c * KC + kpos0), 1.0, 0.0)
            return acc + jnp.sum(hit.reshape(KC // SUBLANES, SUBLANES, TQ), axis=0)
        acc = lax.fori_loop(0, n_kc, body, jnp.zeros((SUBLANES, TQ), F32))
        return jnp.sum(acc, axis=0, keepdims=True)

    def bit_body(i, t):
        cand = t + lax.shift_left(jnp.int32(1), jnp.int32(31) - i)
        cnt = count(lambda kc, pos: kc >= cand)
        return jnp.where(cnt >= float(n_sel), cand, t)

    t = lax.fori_loop(0, 32, bit_body, jnp.full((1, TQ), INT_MIN, I32))
    c_gt = count(lambda kc, pos: kc > t)
    c_ge = count(lambda kc, pos: kc >= t)
    need = float(n_sel) - c_gt
    has_row = t != jnp.int32(INT_MIN)
    midx_ref[...] = jnp.broadcast_to(jnp.where(has_row, jnp.int32(seq), jnp.int32(-1)), midx_ref.shape)
    tie = jnp.where(has_row & (c_ge > float(n_sel)), 1.0, 0.0)

    @pl.when(jnp.max(tie) > 0.0)
    def _():
        def jbit(i, m):
            cand = m + lax.shift_left(jnp.int32(1), jnp.int32(11) - i)
            cnt = count(lambda kc, pos: (kc == t) & (pos < cand))
            return jnp.where(cnt < need, cand, m)
        m = lax.fori_loop(0, 12, jbit, jnp.zeros((1, TQ), I32))
        midx_ref[...] = jnp.broadcast_to(jnp.where(has_row, m, jnp.int32(-1)), midx_ref.shape)

    midx = midx_ref[0:1, :]

    acc_ref[...] = jnp.zeros(acc_ref.shape, F32)

    def att_body(c, stats):
        m_old, l_old = stats
        off = pl.multiple_of(c * KC, KC)
        kc = keys_ref[c]
        sel = (kc > t) | ((kc == t) & (off + kpos0 <= midx))
        bias = jnp.where(sel, 0.0, NEG_BIG)
        kk = k_ref[pl.ds(off, KC), :]
        vt_aug = _with_ones_rows(vt_ref[:, pl.ds(off, KC)])
        s_t = lax.dot_general(kk, q_all, NT_DIMS, preferred_element_type=F32)
        s_t = s_t + jnp.concatenate([bias] * A_HEADS, axis=1)
        m_new, alpha, p_t = _softmax_tile(s_t, m_old)
        pv = jnp.dot(vt_aug, p_t, preferred_element_type=F32)
        acc_ref[...] = alpha * acc_ref[...] + pv[:A_HEAD_DIM]
        return m_new, alpha * l_old + pv[A_HEAD_DIM:A_HEAD_DIM + 1]

    _, l_fin = lax.fori_loop(0, n_kc, att_body,
                             (jnp.full((1, NQ), NEG_BIG, F32), jnp.zeros((1, NQ), F32)))
    o_t = acc_ref[...] / l_fin
    for h in range(A_HEADS):
        out_ref[h * A_HEAD_DIM:(h + 1) * A_HEAD_DIM, :] = o_t[:, h * TQ:(h + 1) * TQ]
    o_ref[...] = out_ref[...].T.astype(BF16)


def _dsa(aq, qi, wit, ki, k, vt, batch, seq):
    T = aq.shape[1]
    TQ, KC = DSA_TQ, DSA_KC
    nq = seq // TQ
    n_sel = min(TOPK_MAX, seq // 4)
    heads = pl.BlockSpec((A_HEADS, TQ, A_HEAD_DIM), lambda b, q: (0, b * nq + q, 0))
    whole = lambda n: pl.BlockSpec((seq, n), lambda b, q: (b, 0))
    return pl.pallas_call(
        functools.partial(_dsa_kernel, seq=seq, n_sel=n_sel),
        grid=(batch, nq),
        in_specs=[heads, heads,
                  pl.BlockSpec((IDX_HEADS, TQ), lambda b, q: (0, b * nq + q)),
                  whole(64), whole(64),
                  pl.BlockSpec((A_HEAD_DIM, seq), lambda b, q: (0, b))],
        out_specs=pl.BlockSpec((TQ, A_HEADS * A_HEAD_DIM), lambda b, q: (b * nq + q, 0)),
        out_shape=jax.ShapeDtypeStruct((T, A_HEADS * A_HEAD_DIM), BF16),
        scratch_shapes=[pltpu.VMEM((seq // KC, KC, TQ), I32),
                        pltpu.VMEM((SUBLANES, TQ), I32),
                        pltpu.VMEM((A_HEAD_DIM, A_HEADS * TQ), F32),
                        pltpu.VMEM((A_HEADS * A_HEAD_DIM, TQ), F32)],
        compiler_params=_cparams(("parallel", "arbitrary")),
        name="dsa",
    )(aq, qi, wit, ki, k, vt)


GLA_ROWS = 2


def _gla_kernel(q_ref, k_ref, v_ref, la_ref, r_ref, ng_ref, o_ref, st_ref, *, seq):
    C = B_CHUNK
    half = C // 2
    row = lax.broadcasted_iota(I32, (C, C), 0)
    colm = lax.broadcasted_iota(I32, (C, C), 1)
    tril = row >= colm
    trilf = jnp.where(tril, 1.0, 0.0)
    st_ref[...] = jnp.zeros(st_ref.shape, F32)

    def chunk(rows, n):
        la = la_ref[rows, :]
        b = jnp.dot(trilf, la, preferred_element_type=F32, precision=lax.Precision.HIGHEST)
        b_mid = b[half - 1:half, :]
        b_last = b[C - 1:C, :]
        q = q_ref[rows, :].astype(F32)
        k = k_ref[rows, :].astype(F32)
        q_in = (q * jnp.exp(b - b_mid)).astype(BF16)
        k_in = (k * jnp.exp(b_mid - b)).astype(BF16)
        q_st = (q * jnp.exp(b)).astype(BF16)
        k_st = (k * jnp.exp(b_last - b)).astype(BF16)
        e_last = jnp.exp(b_last)
        for h in range(B_HEADS):
            ks = slice(h * B_DK, (h + 1) * B_DK)
            vs = slice(h * B_DV, (h + 1) * B_DV)
            a = lax.dot_general(q_in[:, ks], k_in[:, ks], NT_DIMS, preferred_element_type=F32)
            a = jnp.where(tril, a, 0.0).astype(BF16)
            vh = v_ref[rows, vs]
            st = st_ref[n, h]
            o = (jnp.dot(a, vh, preferred_element_type=F32)
                 + lax.dot_general(q_st[:, ks], st.astype(BF16), NT_DIMS, preferred_element_type=F32))
            st_ref[n, h] = st * e_last[:, ks] + lax.dot_general(vh, k_st[:, ks], TN_DIMS,
                                                               preferred_element_type=F32)
            r = r_ref[rows, vs].astype(F32)
            o_ref[rows, vs] = (_rms_norm(o, ng_ref[h:h + 1, :]) * (r * _sigmoid(r))).astype(BF16)

    def body(c, carry):
        for n in range(GLA_ROWS):
            chunk(pl.ds(pl.multiple_of(n * seq + c * C, C), C), n)
        return carry

    lax.fori_loop(0, seq // C, body, 0)


def _gla(bq, bk, bv, la, br, norm_g, batch, seq):
    T = bq.shape[0]
    blk = lambda n: pl.BlockSpec((GLA_ROWS * seq, n), lambda b: (b, 0))
    return pl.pallas_call(
        functools.partial(_gla_kernel, seq=seq),
        grid=(batch // GLA_ROWS,),
        in_specs=[blk(256), blk(256), blk(512), blk(256), blk(512), _full((B_HEADS, B_DV))],
        out_specs=blk(512),
        out_shape=jax.ShapeDtypeStruct((T, 512), BF16),
        scratch_shapes=[pltpu.VMEM((GLA_ROWS, B_HEADS, B_DV, B_DK), F32)],
        compiler_params=_cparams(("parallel",)),
        name="gla",
    )(bq, bk, bv, la, br, norm_g)


def _proj_ln_kernel(*refs, n_in, router):
    acts = refs[:n_in]
    ws = refs[n_in:2 * n_in]
    x_ref, g_ref, b_ref = refs[2 * n_in:2 * n_in + 3]
    rest = refs[2 * n_in + 3:]
    h = jnp.dot(acts[0][...], ws[0][...], preferred_element_type=F32)
    for a, w in zip(acts[1:], ws[1:]):
        h = h + jnp.dot(a[...], w[...], preferred_element_type=F32)
    y = _layer_norm(DN_ALPHA * x_ref[...] + h, g_ref[...], b_ref[...])
    if not router:
        rest[0][...] = y
        return
    wr_ref, br_ref, o_ref, idx_ref, gate_ref = rest
    o_ref[...] = y
    y_hi = y.astype(BF16)
    y_lo = (y - y_hi.astype(F32)).astype(BF16)
    hi = jnp.dot(y_hi, wr_ref[...], preferred_element_type=F32)
    lo = jnp.dot(y_lo, wr_ref[:, 0:N_EXPERTS], preferred_element_type=F32)
    logits = hi[:, 0:N_EXPERTS] + hi[:, N_EXPERTS:2 * N_EXPERTS] + lo + br_ref[...]
    tm = logits.shape[0]
    eid = lax.broadcasted_iota(I32, (tm, N_EXPERTS), 1)
    m1 = jnp.max(logits, axis=1, keepdims=True)
    i1 = jnp.min(jnp.where(logits == m1, eid, N_EXPERTS), axis=1, keepdims=True)
    rest_l = jnp.where(eid == i1, -jnp.inf, logits)
    m2 = jnp.max(rest_l, axis=1, keepdims=True)
    i2 = jnp.min(jnp.where(rest_l == m2, eid, N_EXPERTS), axis=1, keepdims=True)
    e2 = jnp.exp(m2 - m1)
    g1 = 1.0 / (1.0 + e2)
    g2 = e2 / (1.0 + e2)
    two = lax.broadcasted_iota(I32, (tm, 2), 1)
    idx_ref[...] = jnp.where(two == 0, i1, i2)
    gate_ref[...] = jnp.where(two == 0, g1, g2)


def _proj_ln(acts, ws, x, g, b, router=None, tm=512):
    T, D = x.shape
    n_in = len(acts)
    act_specs = [pl.BlockSpec((tm, a.shape[1]), lambda i: (i, 0)) for a in acts]
    in_specs = (act_specs + [_full(w.shape) for w in ws]
                + [pl.BlockSpec((tm, D), lambda i: (i, 0)), _full((1, D)), _full((1, D))])
    args = list(acts) + [w.astype(BF16) for w in ws] + [x, g.reshape(1, D), b.reshape(1, D)]
    out_specs = [pl.BlockSpec((tm, D), lambda i: (i, 0))]
    out_shape = [jax.ShapeDtypeStruct((T, D), F32)]
    if router is not None:
        w_r, b_r = router
        w_hi = w_r.astype(BF16)
        w_lo = (w_r - w_hi.astype(F32)).astype(BF16)
        in_specs += [_full((D, 2 * N_EXPERTS)), _full((1, N_EXPERTS))]
        args += [jnp.concatenate([w_hi, w_lo], axis=1), b_r.reshape(1, N_EXPERTS)]
        out_specs += [pl.BlockSpec((tm, 2), lambda i: (i, 0))] * 2
        out_shape += [jax.ShapeDtypeStruct((T, 2), I32), jax.ShapeDtypeStruct((T, 2), F32)]
    res = pl.pallas_call(
        functools.partial(_proj_ln_kernel, n_in=n_in, router=router is not None),
        grid=(T // tm,),
        in_specs=in_specs, out_specs=out_specs, out_shape=out_shape,
        compiler_params=_cparams(("parallel",)),
        name="proj_ln_router" if router is not None else "proj_ln",
    )(*args)
    return res if router is not None else res[0]


FFN_CHUNK = 512


def _swiglu_rows(x_ref, wg_ref, wu_ref, wd_ref, hid_ref):
    xb = x_ref[...].astype(BF16)
    n_hidden = wg_ref.shape[-1]
    for lo in range(0, n_hidden, FFN_CHUNK):
        cs = slice(lo, min(lo + FFN_CHUNK, n_hidden))
        gt = jnp.dot(xb, wg_ref[:, cs], preferred_element_type=F32)
        up = jnp.dot(xb, wu_ref[:, cs], preferred_element_type=F32)
        hid_ref[:, cs] = (gt * _sigmoid(gt) * up).astype(BF16)
    return jnp.dot(hid_ref[...], wd_ref[...], preferred_element_type=F32)


def _ffn_ln_kernel(x_ref, wg_ref, wu_ref, wd_ref, g_ref, b_ref, o_ref, hid_ref):
    h = _swiglu_rows(x_ref, wg_ref, wu_ref, wd_ref, hid_ref)
    o_ref[...] = _layer_norm(DN_ALPHA * x_ref[...] + h, g_ref[...], b_ref[...])


def _resident(shape, index_map):
    return pl.BlockSpec(shape, index_map, pipeline_mode=pl.Buffered(1))


def _ffn_ln(x, w_gate, w_up, w_down, g, b, tm=512):
    T, D = x.shape
    F = w_gate.shape[1]
    return pl.pallas_call(
        _ffn_ln_kernel,
        grid=(T // tm,),
        in_specs=[pl.BlockSpec((tm, D), lambda i: (i, 0)),
                  _resident((D, F), lambda i: (0, 0)),
                  _resident((D, F), lambda i: (0, 0)),
                  _resident((F, D), lambda i: (0, 0)),
                  _full((1, D)), _full((1, D))],
        out_specs=pl.BlockSpec((tm, D), lambda i: (i, 0)),
        out_shape=jax.ShapeDtypeStruct((T, D), F32),
        scratch_shapes=[pltpu.VMEM((tm, F), BF16)],
        compiler_params=_cparams(("parallel",)),
        name="ffn_ln",
    )(x, w_gate.astype(BF16), w_up.astype(BF16), w_down.astype(BF16), g.reshape(1, D), b.reshape(1, D))


def _moe_ffn_kernel(te_ref, x_ref, wg_ref, wu_ref, wd_ref, o_ref, hid_ref):
    del te_ref
    o_ref[...] = _swiglu_rows(x_ref, wg_ref, wu_ref, wd_ref, hid_ref)


def _moe_ffn(xs, tile_e, w_gate, w_up, w_down, tm):
    R, D = xs.shape
    F = w_gate.shape[2]
    grid_spec = pltpu.PrefetchScalarGridSpec(
        num_scalar_prefetch=1,
        grid=(R // tm,),
        in_specs=[pl.BlockSpec((tm, D), lambda i, te: (i, 0)),
                  _resident((None, D, F), lambda i, te: (te[i], 0, 0)),
                  _resident((None, D, F), lambda i, te: (te[i], 0, 0)),
                  _resident((None, F, D), lambda i, te: (te[i], 0, 0))],
        out_specs=pl.BlockSpec((tm, D), lambda i, te: (i, 0)),
        scratch_shapes=[pltpu.VMEM((tm, F), BF16)],
    )
    return pl.pallas_call(
        _moe_ffn_kernel,
        grid_spec=grid_spec,
        out_shape=jax.ShapeDtypeStruct((R, D), F32),
        compiler_params=_cparams(("arbitrary",)),
        name="moe_ffn",
    )(tile_e, xs, w_gate.astype(BF16), w_up.astype(BF16), w_down.astype(BF16))


MOE_TR = 256


def _row_copy(src_ref, s_row, dst_ref, d_row, sem):
    return pltpu.make_async_copy(src_ref.at[pl.ds(s_row, 1), :], dst_ref.at[pl.ds(d_row, 1), :], sem)


def _moe_scatter_kernel(dest_ref, x_ref, init_ref, o_ref, sem):
    del init_ref
    def issue(r, carry):
        for s in range(2):
            _row_copy(x_ref, r, o_ref, dest_ref[0, 2 * r + s], sem).start()
        return carry
    lax.fori_loop(0, MOE_TR, issue, 0, unroll=8)

    def drain(r, carry):
        for s in range(2):
            _row_copy(x_ref, 0, o_ref, 0, sem).wait()
        return carry
    lax.fori_loop(0, MOE_TR, drain, 0, unroll=8)


def _moe_scatter(x, dest3, n_rows):
    T, D = x.shape
    nt = T // MOE_TR
    return pl.pallas_call(
        _moe_scatter_kernel,
        grid=(nt,),
        in_specs=[pl.BlockSpec((None, 1, 2 * MOE_TR), lambda i: (i, 0, 0), memory_space=pltpu.SMEM),
                  pl.BlockSpec((MOE_TR, D), lambda i: (i, 0)),
                  pl.BlockSpec(memory_space=pl.ANY)],
        out_specs=pl.BlockSpec(memory_space=pl.ANY),
        out_shape=jax.ShapeDtypeStruct((n_rows, D), F32),
        scratch_shapes=[pltpu.SemaphoreType.DMA(())],
        input_output_aliases={2: 0},
        compiler_params=_cparams(("arbitrary",)),
        name="moe_scatter",
    )(dest3, x, jnp.zeros((n_rows, D), F32))


def _moe_combine_kernel(dest_ref, ys_ref, gate_ref, x_ref, g_ref, b_ref, o_ref, buf0, buf1, sem):
    bufs = (buf0, buf1)
    def issue(r, carry):
        for s in range(2):
            _row_copy(ys_ref, dest_ref[0, 2 * r + s], bufs[s], r, sem).start()
        return carry
    lax.fori_loop(0, MOE_TR, issue, 0, unroll=8)

    def drain(r, carry):
        for s in range(2):
            _row_copy(ys_ref, 0, bufs[s], 0, sem).wait()
        return carry
    lax.fori_loop(0, MOE_TR, drain, 0, unroll=8)

    gate = gate_ref[...]
    y = gate[:, 0:1] * buf0[...] + gate[:, 1:2] * buf1[...]
    o_ref[...] = _layer_norm(DN_ALPHA * x_ref[...] + y, g_ref[...], b_ref[...])


def _moe_combine(ys, dest3, gate, x, g, b):
    T, D = x.shape
    nt = T // MOE_TR
    return pl.pallas_call(
        _moe_combine_kernel,
        grid=(nt,),
        in_specs=[pl.BlockSpec((None, 1, 2 * MOE_TR), lambda i: (i, 0, 0), memory_space=pltpu.SMEM),
                  pl.BlockSpec(memory_space=pl.ANY),
                  pl.BlockSpec((MOE_TR, 2), lambda i: (i, 0)),
                  pl.BlockSpec((MOE_TR, D), lambda i: (i, 0)),
                  _full((1, D)), _full((1, D))],
        out_specs=pl.BlockSpec((MOE_TR, D), lambda i: (i, 0)),
        out_shape=jax.ShapeDtypeStruct((T, D), F32),
        scratch_shapes=[pltpu.VMEM((MOE_TR, D), F32), pltpu.VMEM((MOE_TR, D), F32),
                        pltpu.SemaphoreType.DMA(())],
        compiler_params=_cparams(("arbitrary",)),
        name="moe_combine",
    )(dest3, ys, gate, x, g.reshape(1, D), b.reshape(1, D))


def _moe(x, idx, gate, w_gate, w_up, w_down, g, b, tm=1024):
    T, D = x.shape
    pairs = 2 * T
    e = idx.reshape(pairs)
    onehot = (e[:, None] == jnp.arange(N_EXPERTS, dtype=I32)[None, :]).astype(I32)
    csum = jnp.cumsum(onehot, axis=0)
    rank = jnp.sum((csum - onehot) * onehot, axis=1)
    counts = csum[-1]
    padded = ((counts + tm - 1) // tm) * tm
    ends = jnp.cumsum(padded)
    starts = ends - padded
    dest = jnp.sum(onehot * starts[None, :], axis=1) + rank
    n_rows = pairs + N_EXPERTS * tm
    tiles = jnp.arange(n_rows // tm, dtype=I32) * tm
    tile_e = jnp.minimum(jnp.sum((tiles[:, None] >= ends[None, :]).astype(I32), axis=1), N_EXPERTS - 1)
    dest3 = dest.astype(I32).reshape(T // MOE_TR, 1, 2 * MOE_TR)
    xs = _moe_scatter(x, dest3, n_rows)
    ys = _moe_ffn(xs, tile_e.astype(I32), w_gate, w_up, w_down, tm)
    return _moe_combine(ys, dest3, gate, x, g, b)


DIFF_TQ = 512
DIFF_HP = 4


def _qkv_kernel(x_ref, w_ref, q_ref, k_ref, vt_ref):
    xb = x_ref[...].astype(BF16)
    n = q_ref.shape[1]
    hd = n // C_HEADS
    q_ref[...] = (jnp.dot(xb, w_ref[:, 0:n], preferred_element_type=F32)
                  * (C_QK_DIM ** -0.5 * LOG2E)).astype(BF16)
    k_ref[...] = jnp.dot(xb, w_ref[:, n:2 * n], preferred_element_type=F32).astype(BF16)
    for h in range(C_HEADS):
        v = jnp.dot(xb, w_ref[:, 2 * n + h * hd:2 * n + (h + 1) * hd], preferred_element_type=F32)
        vt_ref[0, h] = v.astype(BF16).T


def _qkv(x, w, batch, seq, tm=512):
    T, D = x.shape
    n = w.shape[1] // 3
    hd = n // C_HEADS
    per = seq // tm
    rows = pl.BlockSpec((tm, n), lambda i: (i, 0))
    return pl.pallas_call(
        _qkv_kernel,
        grid=(T // tm,),
        in_specs=[pl.BlockSpec((tm, D), lambda i: (i, 0)), _full(w.shape)],
        out_specs=[rows, rows, pl.BlockSpec((1, C_HEADS, hd, tm), lambda i: (i // per, 0, 0, i % per))],
        out_shape=[jax.ShapeDtypeStruct((T, n), BF16)] * 2
                  + [jax.ShapeDtypeStruct((batch, C_HEADS, hd, seq), BF16)],
        compiler_params=_cparams(("parallel",)),
        name="qkv",
    )(x, w.astype(BF16))


def _diff_kernel(lam_ref, q_ref, k_ref, vt_ref, g_ref, o_ref, acc_ref, *, lambda_init):
    TQ = DIFF_TQ
    hd = 2 * C_QK_DIM
    qi = pl.program_id(2)
    kidx = lax.broadcasted_iota(I32, (TQ, TQ), 0)
    qidx = lax.broadcasted_iota(I32, (TQ, TQ), 1)
    lam_v = lam_ref[...]
    lam = (jnp.exp(jnp.sum(lam_v[0:1] * lam_v[1:2], axis=1, keepdims=True))
           - jnp.exp(jnp.sum(lam_v[2:3] * lam_v[3:4], axis=1, keepdims=True)) + lambda_init)

    def step(c, stats, diagonal):
        keys = pl.ds(pl.multiple_of(c * TQ, TQ), TQ)
        out = []
        for i in range(DIFF_HP):
            vt_aug = _with_ones_rows(vt_ref[0, i, :, keys])
            for j in range(2):
                n = 2 * i + j
                sl = slice(i * hd + j * C_QK_DIM, i * hd + (j + 1) * C_QK_DIM)
                s_t = lax.dot_general(k_ref[keys, sl], q_ref[:, sl], NT_DIMS,
                                      preferred_element_type=F32)
                if diagonal:
                    s_t = jnp.where(kidx <= qidx, s_t, NEG_BIG)
                m_new, alpha, p_t = _softmax_tile(s_t, stats[2 * n])
                pv = jnp.dot(vt_aug, p_t, preferred_element_type=F32)
                acc_ref[n] = alpha * acc_ref[n] + pv[:C_V_DIM]
                out += [m_new, alpha * stats[2 * n + 1] + pv[C_V_DIM:C_V_DIM + 1]]
        return tuple(out)

    acc_ref[...] = jnp.zeros(acc_ref.shape, F32)
    neg = jnp.full((1, TQ), NEG_BIG, F32)
    zero = jnp.zeros((1, TQ), F32)
    stats = lax.fori_loop(0, qi, lambda c, st: step(c, st, False), (neg, zero) * (2 * DIFF_HP))
    stats = step(qi, stats, True)
    for i in range(DIFF_HP):
        l0, l1 = stats[4 * i + 1], stats[4 * i + 3]
        o_t = acc_ref[2 * i] / l0 - lam * (acc_ref[2 * i + 1] / l1)
        o_t = o_t * lax.rsqrt(jnp.mean(o_t * o_t, axis=0, keepdims=True) + RMS_EPS)
        o_ref[:, i * C_V_DIM:(i + 1) * C_V_DIM] = (o_t.T * g_ref[...] * (1.0 - lambda_init)).astype(BF16)


def _diff_attn(q, k, vt, lam4, subln_g, lambda_init, batch, seq):
    T, n = q.shape
    TQ = DIFF_TQ
    nq = seq // TQ
    wide = DIFF_HP * 2 * C_QK_DIM
    return pl.pallas_call(
        functools.partial(_diff_kernel, lambda_init=lambda_init),
        grid=(batch, C_HEADS // DIFF_HP, nq),
        in_specs=[_full((4, C_QK_DIM)),
                  pl.BlockSpec((TQ, wide), lambda b, hg, i: (b * nq + i, hg)),
                  pl.BlockSpec((seq, wide), lambda b, hg, i: (b, hg)),
                  pl.BlockSpec((1, DIFF_HP, C_V_DIM, seq), lambda b, hg, i: (b, hg, 0, 0)),
                  _full((1, C_V_DIM))],
        out_specs=pl.BlockSpec((TQ, DIFF_HP * C_V_DIM), lambda b, hg, i: (b * nq + i, hg)),
        out_shape=jax.ShapeDtypeStruct((T, C_HEADS * C_V_DIM), BF16),
        scratch_shapes=[pltpu.VMEM((2 * DIFF_HP, C_V_DIM, TQ), F32)],
        compiler_params=_cparams(("parallel", "parallel", "arbitrary")),
        name="diff_attn",
    )(lam4, q, k, vt, subln_g.reshape(1, C_V_DIM))


def _even_layer(x, p, j, batch, seq):
    (aq, qi, bq, bk, bv, br, k, vt, ki, wit, la) = _even_in(
        x, p['ev_w_in'][j], p['ev_a_kv_norm_g'][j], p['ev_a_w_uk'][j], p['ev_a_w_uv'][j],
        p['ev_a_kidx_ln_g'][j], p['ev_a_kidx_ln_b'][j], p['ev_b_w_g2'][j], p['ev_b_b_g'][j])
    a_out = _dsa(aq, qi, wit, ki, k, vt, batch, seq)
    b_out = _gla(bq, bk, bv, la, br, p['ev_b_norm_g'][j], batch, seq)
    w_out = p['ev_w_out'][j]
    na = A_HEADS * A_HEAD_DIM
    x = _proj_ln([a_out, b_out], [w_out[:na], w_out[na:]], x, p['ev_ln1_g'][j], p['ev_ln1_b'][j])
    return _ffn_ln(x, p['ev_ffn_w_gate'][j], p['ev_ffn_w_up'][j], p['ev_ffn_w_down'][j],
                   p['ev_ln2_g'][j], p['ev_ln2_b'][j])


def _odd_layer(x, p, j, layer, batch, seq):
    lambda_init = 0.8 - 0.6 * math.exp(-0.3 * layer)
    q, k, vt = _qkv(x, p['od_w_qkv'][j], batch, seq)
    lam4 = jnp.stack([p['od_lam_q1'][j], p['od_lam_k1'][j], p['od_lam_q2'][j], p['od_lam_k2'][j]])
    o = _diff_attn(q, k, vt, lam4, p['od_subln_g'][j], lambda_init, batch, seq)
    x, idx, gate = _proj_ln([o], [p['od_w_out'][j]], x, p['od_ln1_g'][j], p['od_ln1_b'][j],
                            router=(p['od_router_w'][j], p['od_router_b'][j]))
    return _moe(x, idx, gate, p['od_moe_w_gate'][j], p['od_moe_w_up'][j], p['od_moe_w_down'][j],
                p['od_ln2_g'][j], p['od_ln2_b'][j])


def kernel(x, ev_w_in, ev_a_kv_norm_g, ev_a_w_uk, ev_a_w_uv, ev_a_kidx_ln_g, ev_a_kidx_ln_b,
           ev_b_w_g2, ev_b_b_g, ev_b_norm_g, ev_w_out, ev_ln1_g, ev_ln1_b,
           ev_ffn_w_gate, ev_ffn_w_up, ev_ffn_w_down, ev_ln2_g, ev_ln2_b,
           od_w_qkv, od_lam_q1, od_lam_k1, od_lam_q2, od_lam_k2, od_subln_g, od_w_out,
           od_ln1_g, od_ln1_b, od_router_w, od_router_b,
           od_moe_w_gate, od_moe_w_up, od_moe_w_down, od_ln2_g, od_ln2_b):
    p = dict(locals())
    batch, seq, d = x.shape
    h = x.reshape(batch * seq, d)
    for layer in range(DEPTH):
        j = layer // 2
        if layer % 2 == 0:
            h = _even_layer(h, p, j, batch, seq)
        else:
            h = _odd_layer(h, p, j, layer, batch, seq)
    return h.reshape(batch, seq, d)
```

```python
import functools
import math

import jax
import jax.numpy as jnp
from jax import lax
from jax.experimental import pallas as pl
from jax.experimental.pallas import tpu as pltpu

F32 = jnp.float32
BF16 = jnp.bfloat16
I32 = jnp.int32

DEPTH = 4
DN_ALPHA = (2.0 * DEPTH) ** 0.25
LN_EPS = 1e-5
RMS_EPS = 1e-6

A_HEADS = 8
A_HEAD_DIM = 64
A_KV_LATENT = 128
IDX_HEADS = 8
IDX_DIM = 64
TOPK_MAX = 256

B_HEADS = 4
B_DK = 64
B_DV = 128
B_GATE_RANK = 16
B_GATE_TEMP = 16.0
B_CHUNK = 64

C_HEADS = 8
C_QK_DIM = 64
C_V_DIM = 128

N_EXPERTS = 8

LANES = 128
SUBLANES = 8
WORD_BITS = 32
VMEM_LIMIT = 56 * 1024 * 1024

NEG_BIG = -1e30
INT_MIN = -(2 ** 31)

NT_DIMS = (((1,), (1,)), ((), ()))
TN_DIMS = (((0,), (0,)), ((), ()))


def _cparams(sem):
    return pltpu.CompilerParams(dimension_semantics=sem, vmem_limit_bytes=VMEM_LIMIT)


def _layer_norm(y, g, b):
    mu = jnp.mean(y, axis=-1, keepdims=True)
    d = y - mu
    var = jnp.mean(d * d, axis=-1, keepdims=True)
    return d * lax.rsqrt(var + LN_EPS) * g + b


def _rms_norm(y, g):
    return y * lax.rsqrt(jnp.mean(y * y, axis=-1, keepdims=True) + RMS_EPS) * g


def _sigmoid(z):
    return 1.0 / (1.0 + jnp.exp(-z))


def _full(shape):
    return pl.BlockSpec(shape, lambda *_: (0,) * len(shape))


LOG2E = math.log2(math.e)
ONES_ROWS = 16


def _softmax_tile(s_t, m_old):
    m_new = jnp.maximum(m_old, jnp.max(s_t, axis=0, keepdims=True))
    alpha = jnp.exp2(m_old - m_new)
    p_t = jnp.exp2((s_t - m_new).astype(BF16))
    return m_new, alpha, p_t


def _with_ones_rows(vt):
    return jnp.concatenate([vt, jnp.ones((ONES_ROWS, vt.shape[1]), BF16)], axis=0)


def _even_in_kernel(x_ref, wbig_ref, wsm_ref, wwit_ref, kvg_ref, wuk_ref, wuvt_ref, lng_ref, lnb_ref,
                    wg2_ref, bg2_ref,
                    aq_ref, qi_ref, bq_ref, bk_ref, bv_ref, br_ref, k_ref, vt_ref, ki_ref, wit_ref, la_ref):
    xb = x_ref[...].astype(BF16)

    def proj(lo, hi):
        return jnp.dot(xb, wbig_ref[:, lo:hi], preferred_element_type=F32)

    aq = (proj(0, 512) * (A_HEAD_DIM ** -0.5 * LOG2E)).astype(BF16)
    qi = (proj(512, 1024) * IDX_DIM ** -0.5).astype(BF16)
    for h in range(A_HEADS):
        aq_ref[h] = aq[:, h * A_HEAD_DIM:(h + 1) * A_HEAD_DIM]
        qi_ref[h] = qi[:, h * IDX_DIM:(h + 1) * IDX_DIM]
    bq_ref[...] = (proj(1024, 1280) * B_DK ** -0.5).astype(BF16)
    bk_ref[...] = proj(1280, 1536).astype(BF16)
    bv_ref[...] = proj(1536, 2048).astype(BF16)
    br_ref[...] = proj(2048, 2560).astype(BF16)

    sm = jnp.dot(xb, wsm_ref[...], preferred_element_type=F32)
    ckv = sm[:, 0:128]
    aki = sm[:, 128:192]
    bg = sm[:, 192:208]

    c = _rms_norm(ckv, kvg_ref[...]).astype(BF16)
    k_ref[...] = jnp.dot(c, wuk_ref[...], preferred_element_type=F32).astype(BF16)
    vt_ref[...] = lax.dot_general(wuvt_ref[...], c, NT_DIMS, preferred_element_type=F32).astype(BF16)
    ki_ref[...] = _layer_norm(aki, lng_ref[...], lnb_ref[...]).astype(BF16)
    wit_ref[...] = lax.dot_general(wwit_ref[...], xb, NT_DIMS, preferred_element_type=F32) * IDX_HEADS ** -0.5
    z = jnp.dot(bg, wg2_ref[...], preferred_element_type=F32, precision=lax.Precision.HIGHEST) + bg2_ref[...]
    la_ref[...] = (jnp.minimum(z, 0.0) - jnp.log1p(jnp.exp(-jnp.abs(z)))) * (1.0 / B_GATE_TEMP)


def _even_in(x, w_in, kv_g, w_uk, w_uv, ln_g, ln_b, w_g2, b_g, tm=512):
    T, D = x.shape
    o = [0, 512, 640, 1152, 1216, 1224, 1480, 1736, 2248, 2264, 2776]
    col = lambda i: w_in[:, o[i]:o[i + 1]]
    wbig = jnp.concatenate([col(0), col(2), col(5), col(6), col(7), col(9)], axis=1).astype(BF16)
    wsm = jnp.concatenate([col(1), col(3), col(8), jnp.zeros((D, 48), F32)], axis=1).astype(BF16)
    wwit = col(4).T.astype(BF16)
    rows = lambda n, dt: (pl.BlockSpec((tm, n), lambda i: (i, 0)), jax.ShapeDtypeStruct((T, n), dt))
    cols = lambda n, dt: (pl.BlockSpec((n, tm), lambda i: (0, i)), jax.ShapeDtypeStruct((n, T), dt))
    heads = (pl.BlockSpec((A_HEADS, tm, A_HEAD_DIM), lambda i: (0, i, 0)),
             jax.ShapeDtypeStruct((A_HEADS, T, A_HEAD_DIM), BF16))
    outs = [heads, heads, rows(256, BF16), rows(256, BF16), rows(512, BF16), rows(512, BF16),
            rows(64, BF16), cols(64, BF16), rows(64, BF16), cols(8, F32), rows(256, F32)]
    return pl.pallas_call(
        _even_in_kernel,
        grid=(T // tm,),
        in_specs=[pl.BlockSpec((tm, D), lambda i: (i, 0)),
                  _full((D, 2560)), _full((D, 256)), _full((8, D)), _full((1, 128)), _full((128, 64)),
                  _full((64, 128)), _full((1, 64)), _full((1, 64)), _full((16, 256)), _full((1, 256))],
        out_specs=[s for s, _ in outs],
        out_shape=[s for _, s in outs],
        compiler_params=_cparams(("parallel",)),
        name="even_in",
    )(x, wbig, wsm, wwit, kv_g.reshape(1, 128), w_uk.astype(BF16), w_uv.T.astype(BF16),
      ln_g.reshape(1, 64), ln_b.reshape(1, 64), w_g2, b_g.reshape(1, 256))


DSA_TQ = 256
DSA_KC = 256


def _dsa_kernel(aq_ref, qi_ref, wit_ref, ki_ref, k_ref, vt_ref, o_ref,
                keys_ref, planes_ref, midx_ref, acc_ref, out_ref, *, seq, n_sel):
    TQ, KC = DSA_TQ, DSA_KC
    NQ = A_HEADS * TQ
    q_all = aq_ref[...].reshape(NQ, A_HEAD_DIM)
    qi_all = qi_ref[...].reshape(IDX_HEADS * TQ, IDX_DIM)
    qb = pl.program_id(1)
    n_kc = (qb * TQ + TQ + KC - 1) // KC
    kpos0 = lax.broadcasted_iota(I32, (KC, TQ), 0)
    qpos = qb * TQ + lax.broadcasted_iota(I32, (KC, TQ), 1)
    w_t = wit_ref[...]

    def idx_body(c, carry):
        off = pl.multiple_of(c * KC, KC)
        kic = ki_ref[pl.ds(off, KC), :]
        rel = lax.dot_general(kic, qi_all, NT_DIMS, preferred_element_type=F32)
        score = jnp.zeros((KC, TQ), F32)
        for h in range(IDX_HEADS):
            score = score + w_t[h:h + 1, :] * jnp.maximum(rel[:, h * TQ:(h + 1) * TQ], 0.0)
        score = jnp.where(score == 0.0, 0.0, score)
        bits = lax.bitcast_convert_type(score, I32)
        key = jnp.where(bits < 0, bits ^ jnp.int32(0x7FFFFFFF), bits)
        key = jnp.where(off + kpos0 <= qpos, key, jnp.int32(INT_MIN))
        keys_ref[c] = key
        x = [key[v * SUBLANES:(v + 1) * SUBLANES, :] ^ jnp.int32(INT_MIN) for v in range(WORD_BITS)]
        for j, mask in ((16, 0x0000FFFF), (8, 0x00FF00FF), (4, 0x0F0F0F0F), (2, 0x33333333), (1, 0x55555555)):
            for r in range(WORD_BITS):
                if r & j == 0:
                    tmp = (x[r] ^ lax.shift_right_logical(x[r + j], jnp.int32(j))) & jnp.int32(mask)
                    x[r] = x[r] ^ tmp
                    x[r + j] = x[r + j] ^ lax.shift_left(tmp, jnp.int32(j))
        for i in range(WORD_BITS):
            planes_ref[c, i] = x[i]
        return carry

    lax.fori_loop(0, n_kc, idx_body, 0)
    n_chunks = seq // KC

    def clear_body(c, carry):
        planes_ref[c] = jnp.zeros(planes_ref.shape[1:], I32)
        return carry

    lax.fori_loop(n_kc, n_chunks, clear_body, 0)

    in_play0 = tuple(jnp.where(c < n_kc, jnp.int32(-1), jnp.int32(0)) * jnp.ones((SUBLANES, TQ), I32)
                     for c in range(n_chunks))

    def set_bits(words):
        per_word = functools.reduce(lambda a, b: a + b, [lax.population_count(w) for w in words])
        return jnp.sum(per_word.astype(F32), axis=0, keepdims=True)

    def bit_body(i, carry):
        t_u, need, in_play = carry
        ones = [in_play[c] & planes_ref[c, i] for c in range(n_chunks)]
        cnt = set_bits(ones)
        take = cnt >= need
        in_play = tuple(jnp.where(take, o, p & ~o) for o, p in zip(ones, in_play))
        t_u = jnp.where(take, t_u | lax.shift_left(jnp.int32(1), jnp.int32(31) - i), t_u)
        return t_u, jnp.where(take, need, need - cnt), in_play

    t_u, need, eq_words = lax.fori_loop(
        0, WORD_BITS, bit_body,
        (jnp.zeros((1, TQ), I32), jnp.full((1, TQ), float(n_sel), F32), in_play0))
    t = t_u ^ jnp.int32(INT_MIN)
    c_eq = set_bits(list(eq_words))
    has_row = t != jnp.int32(INT_MIN)
    midx_ref[...] = jnp.broadcast_to(jnp.where(has_row, jnp.int32(seq), jnp.int32(-1)), midx_ref.shape)
    tie = jnp.where(has_row & (c_eq > need), 1.0, 0.0)

    def count(pred):
        def body(c, acc):
            hit = jnp.where(pred(keys_ref[c], c * KC + kpos0), 1.0, 0.0)
            return acc + jnp.sum(hit.reshape(KC // SUBLANES, SUBLANES, TQ), axis=0)
        acc = lax.fori_loop(0, n_kc, body, jnp.zeros((SUBLANES, TQ), F32))
        return jnp.sum(acc, axis=0, keepdims=True)

    @pl.when(jnp.max(tie) > 0.0)
    def _():
        def jbit(i, m):
            cand = m + lax.shift_left(jnp.int32(1), jnp.int32(11) - i)
            cnt = count(lambda kc, pos: (kc == t) & (pos < cand))
            return jnp.where(cnt < need, cand, m)
        m = lax.fori_loop(0, 12, jbit, jnp.zeros((1, TQ), I32))
        midx_ref[...] = jnp.broadcast_to(jnp.where(has_row, m, jnp.int32(-1)), midx_ref.shape)

    midx = midx_ref[0:1, :]

    acc_ref[...] = jnp.zeros(acc_ref.shape, F32)

    def att_body(c, stats):
        m_old, l_old = stats
        off = pl.multiple_of(c * KC, KC)
        kc = keys_ref[c]
        sel = (kc > t) | ((kc == t) & (off + kpos0 <= midx))
        bias = jnp.where(sel, 0.0, NEG_BIG)
        kk = k_ref[pl.ds(off, KC), :]
        vt_aug = _with_ones_rows(vt_ref[:, pl.ds(off, KC)])
        s_t = lax.dot_general(kk, q_all, NT_DIMS, preferred_element_type=F32)
        s_t = s_t + jnp.concatenate([bias] * A_HEADS, axis=1)
        m_new, alpha, p_t = _softmax_tile(s_t, m_old)
        pv = jnp.dot(vt_aug, p_t, preferred_element_type=F32)
        acc_ref[...] = alpha * acc_ref[...] + pv[:A_HEAD_DIM]
        return m_new, alpha * l_old + pv[A_HEAD_DIM:A_HEAD_DIM + 1]

    _, l_fin = lax.fori_loop(0, n_kc, att_body,
                             (jnp.full((1, NQ), NEG_BIG, F32), jnp.zeros((1, NQ), F32)))
    o_t = acc_ref[...] / l_fin
    for h in range(A_HEADS):
        out_ref[h * A_HEAD_DIM:(h + 1) * A_HEAD_DIM, :] = o_t[:, h * TQ:(h + 1) * TQ]
    o_ref[...] = out_ref[...].T.astype(BF16)


def _dsa(aq, qi, wit, ki, k, vt, batch, seq):
    T = aq.shape[1]
    TQ, KC = DSA_TQ, DSA_KC
    assert KC == WORD_BITS * SUBLANES
    nq = seq // TQ
    n_sel = min(TOPK_MAX, seq // 4)
    heads = pl.BlockSpec((A_HEADS, TQ, A_HEAD_DIM), lambda b, q: (0, b * nq + q, 0))
    whole = lambda n: pl.BlockSpec((seq, n), lambda b, q: (b, 0))
    return pl.pallas_call(
        functools.partial(_dsa_kernel, seq=seq, n_sel=n_sel),
        grid=(batch, nq),
        in_specs=[heads, heads,
                  pl.BlockSpec((IDX_HEADS, TQ), lambda b, q: (0, b * nq + q)),
                  whole(64), whole(64),
                  pl.BlockSpec((A_HEAD_DIM, seq), lambda b, q: (0, b))],
        out_specs=pl.BlockSpec((TQ, A_HEADS * A_HEAD_DIM), lambda b, q: (b * nq + q, 0)),
        out_shape=jax.ShapeDtypeStruct((T, A_HEADS * A_HEAD_DIM), BF16),
        scratch_shapes=[pltpu.VMEM((seq // KC, KC, TQ), I32),
                        pltpu.VMEM((seq // KC, WORD_BITS, SUBLANES, TQ), I32),
                        pltpu.VMEM((SUBLANES, TQ), I32),
                        pltpu.VMEM((A_HEAD_DIM, A_HEADS * TQ), F32),
                        pltpu.VMEM((A_HEADS * A_HEAD_DIM, TQ), F32)],
        compiler_params=_cparams(("parallel", "arbitrary")),
        name="dsa",
    )(aq, qi, wit, ki, k, vt)


GLA_ROWS = 2


def _gla_kernel(q_ref, k_ref, v_ref, la_ref, r_ref, ng_ref, o_ref, st_ref, *, seq):
    C = B_CHUNK
    half = C // 2
    row = lax.broadcasted_iota(I32, (C, C), 0)
    colm = lax.broadcasted_iota(I32, (C, C), 1)
    tril = row >= colm
    trilf = jnp.where(tril, 1.0, 0.0)
    st_ref[...] = jnp.zeros(st_ref.shape, F32)

    def chunk(rows, n):
        la = la_ref[rows, :]
        b = jnp.dot(trilf, la, preferred_element_type=F32, precision=lax.Precision.HIGHEST)
        b_mid = b[half - 1:half, :]
        b_last = b[C - 1:C, :]
        q = q_ref[rows, :].astype(F32)
        k = k_ref[rows, :].astype(F32)
        q_in = (q * jnp.exp(b - b_mid)).astype(BF16)
        k_in = (k * jnp.exp(b_mid - b)).astype(BF16)
        q_st = (q * jnp.exp(b)).astype(BF16)
        k_st = (k * jnp.exp(b_last - b)).astype(BF16)
        e_last = jnp.exp(b_last)
        for h in range(B_HEADS):
            ks = slice(h * B_DK, (h + 1) * B_DK)
            vs = slice(h * B_DV, (h + 1) * B_DV)
            a = lax.dot_general(q_in[:, ks], k_in[:, ks], NT_DIMS, preferred_element_type=F32)
            a = jnp.where(tril, a, 0.0).astype(BF16)
            vh = v_ref[rows, vs]
            st = st_ref[n, h]
            o = (jnp.dot(a, vh, preferred_element_type=F32)
                 + lax.dot_general(q_st[:, ks], st.astype(BF16), NT_DIMS, preferred_element_type=F32))
            st_ref[n, h] = st * e_last[:, ks] + lax.dot_general(vh, k_st[:, ks], TN_DIMS,
                                                               preferred_element_type=F32)
            r = r_ref[rows, vs].astype(F32)
            o_ref[rows, vs] = (_rms_norm(o, ng_ref[h:h + 1, :]) * (r * _sigmoid(r))).astype(BF16)

    def body(c, carry):
        for n in range(GLA_ROWS):
            chunk(pl.ds(pl.multiple_of(n * seq + c * C, C), C), n)
        return carry

    lax.fori_loop(0, seq // C, body, 0)


def _gla(bq, bk, bv, la, br, norm_g, batch, seq):
    T = bq.shape[0]
    blk = lambda n: pl.BlockSpec((GLA_ROWS * seq, n), lambda b: (b, 0))
    return pl.pallas_call(
        functools.partial(_gla_kernel, seq=seq),
        grid=(batch // GLA_ROWS,),
        in_specs=[blk(256), blk(256), blk(512), blk(256), blk(512), _full((B_HEADS, B_DV))],
        out_specs=blk(512),
        out_shape=jax.ShapeDtypeStruct((T, 512), BF16),
        scratch_shapes=[pltpu.VMEM((GLA_ROWS, B_HEADS, B_DV, B_DK), F32)],
        compiler_params=_cparams(("parallel",)),
        name="gla",
    )(bq, bk, bv, la, br, norm_g)


def _proj_ln_kernel(*refs, n_in, router):
    acts = refs[:n_in]
    ws = refs[n_in:2 * n_in]
    x_ref, g_ref, b_ref = refs[2 * n_in:2 * n_in + 3]
    rest = refs[2 * n_in + 3:]
    h = jnp.dot(acts[0][...], ws[0][...], preferred_element_type=F32)
    for a, w in zip(acts[1:], ws[1:]):
        h = h + jnp.dot(a[...], w[...], preferred_element_type=F32)
    y = _layer_norm(DN_ALPHA * x_ref[...] + h, g_ref[...], b_ref[...])
    if not router:
        rest[0][...] = y
        return
    wr_ref, br_ref, o_ref, idx_ref, gate_ref = rest
    o_ref[...] = y
    y_hi = y.astype(BF16)
    y_lo = (y - y_hi.astype(F32)).astype(BF16)
    hi = jnp.dot(y_hi, wr_ref[...], preferred_element_type=F32)
    lo = jnp.dot(y_lo, wr_ref[:, 0:N_EXPERTS], preferred_element_type=F32)
    logits = hi[:, 0:N_EXPERTS] + hi[:, N_EXPERTS:2 * N_EXPERTS] + lo + br_ref[...]
    tm = logits.shape[0]
    eid = lax.broadcasted_iota(I32, (tm, N_EXPERTS), 1)
    m1 = jnp.max(logits, axis=1, keepdims=True)
    i1 = jnp.min(jnp.where(logits == m1, eid, N_EXPERTS), axis=1, keepdims=True)
    rest_l = jnp.where(eid == i1, -jnp.inf, logits)
    m2 = jnp.max(rest_l, axis=1, keepdims=True)
    i2 = jnp.min(jnp.where(rest_l == m2, eid, N_EXPERTS), axis=1, keepdims=True)
    e2 = jnp.exp(m2 - m1)
    g1 = 1.0 / (1.0 + e2)
    g2 = e2 / (1.0 + e2)
    two = lax.broadcasted_iota(I32, (tm, 2), 1)
    idx_ref[...] = jnp.where(two == 0, i1, i2)
    gate_ref[...] = jnp.where(two == 0, g1, g2)


def _proj_ln(acts, ws, x, g, b, router=None, tm=512):
    T, D = x.shape
    n_in = len(acts)
    act_specs = [pl.BlockSpec((tm, a.shape[1]), lambda i: (i, 0)) for a in acts]
    in_specs = (act_specs + [_full(w.shape) for w in ws]
                + [pl.BlockSpec((tm, D), lambda i: (i, 0)), _full((1, D)), _full((1, D))])
    args = list(acts) + [w.astype(BF16) for w in ws] + [x, g.reshape(1, D), b.reshape(1, D)]
    out_specs = [pl.BlockSpec((tm, D), lambda i: (i, 0))]
    out_shape = [jax.ShapeDtypeStruct((T, D), F32)]
    if router is not None:
        w_r, b_r = router
        w_hi = w_r.astype(BF16)
        w_lo = (w_r - w_hi.astype(F32)).astype(BF16)
        in_specs += [_full((D, 2 * N_EXPERTS)), _full((1, N_EXPERTS))]
        args += [jnp.concatenate([w_hi, w_lo], axis=1), b_r.reshape(1, N_EXPERTS)]
        out_specs += [pl.BlockSpec((tm, 2), lambda i: (i, 0))] * 2
        out_shape += [jax.ShapeDtypeStruct((T, 2), I32), jax.ShapeDtypeStruct((T, 2), F32)]
    res = pl.pallas_call(
        functools.partial(_proj_ln_kernel, n_in=n_in, router=router is not None),
        grid=(T // tm,),
        in_specs=in_specs, out_specs=out_specs, out_shape=out_shape,
        compiler_params=_cparams(("parallel",)),
        name="proj_ln_router" if router is not None else "proj_ln",
    )(*args)
    return res if router is not None else res[0]


FFN_CHUNK = 512


def _swiglu_rows(x_ref, wg_ref, wu_ref, wd_ref, hid_ref):
    xb = x_ref[...].astype(BF16)
    n_hidden = wg_ref.shape[-1]
    for lo in range(0, n_hidden, FFN_CHUNK):
        cs = slice(lo, min(lo + FFN_CHUNK, n_hidden))
        gt = jnp.dot(xb, wg_ref[:, cs], preferred_element_type=F32)
        up = jnp.dot(xb, wu_ref[:, cs], preferred_element_type=F32)
        hid_ref[:, cs] = (gt * _sigmoid(gt) * up).astype(BF16)
    return jnp.dot(hid_ref[...], wd_ref[...], preferred_element_type=F32)


def _ffn_ln_kernel(x_ref, wg_ref, wu_ref, wd_ref, g_ref, b_ref, o_ref, hid_ref):
    h = _swiglu_rows(x_ref, wg_ref, wu_ref, wd_ref, hid_ref)
    o_ref[...] = _layer_norm(DN_ALPHA * x_ref[...] + h, g_ref[...], b_ref[...])


def _resident(shape, index_map):
    return pl.BlockSpec(shape, index_map, pipeline_mode=pl.Buffered(1))


def _ffn_ln(x, w_gate, w_up, w_down, g, b, tm=512):
    T, D = x.shape
    F = w_gate.shape[1]
    return pl.pallas_call(
        _ffn_ln_kernel,
        grid=(T // tm,),
        in_specs=[pl.BlockSpec((tm, D), lambda i: (i, 0)),
                  _resident((D, F), lambda i: (0, 0)),
                  _resident((D, F), lambda i: (0, 0)),
                  _resident((F, D), lambda i: (0, 0)),
                  _full((1, D)), _full((1, D))],
        out_specs=pl.BlockSpec((tm, D), lambda i: (i, 0)),
        out_shape=jax.ShapeDtypeStruct((T, D), F32),
        scratch_shapes=[pltpu.VMEM((tm, F), BF16)],
        compiler_params=_cparams(("parallel",)),
        name="ffn_ln",
    )(x, w_gate.astype(BF16), w_up.astype(BF16), w_down.astype(BF16), g.reshape(1, D), b.reshape(1, D))


def _moe_ffn_kernel(te_ref, x_ref, wg_ref, wu_ref, wd_ref, o_ref, hid_ref):
    del te_ref
    o_ref[...] = _swiglu_rows(x_ref, wg_ref, wu_ref, wd_ref, hid_ref)


def _moe_ffn(xs, tile_e, w_gate, w_up, w_down, tm):
    R, D = xs.shape
    F = w_gate.shape[2]
    grid_spec = pltpu.PrefetchScalarGridSpec(
        num_scalar_prefetch=1,
        grid=(R // tm,),
        in_specs=[pl.BlockSpec((tm, D), lambda i, te: (i, 0)),
                  _resident((None, D, F), lambda i, te: (te[i], 0, 0)),
                  _resident((None, D, F), lambda i, te: (te[i], 0, 0)),
                  _resident((None, F, D), lambda i, te: (te[i], 0, 0))],
        out_specs=pl.BlockSpec((tm, D), lambda i, te: (i, 0)),
        scratch_shapes=[pltpu.VMEM((tm, F), BF16)],
    )
    return pl.pallas_call(
        _moe_ffn_kernel,
        grid_spec=grid_spec,
        out_shape=jax.ShapeDtypeStruct((R, D), F32),
        compiler_params=_cparams(("arbitrary",)),
        name="moe_ffn",
    )(tile_e, xs, w_gate.astype(BF16), w_up.astype(BF16), w_down.astype(BF16))


MOE_TR = 256


def _row_copy(src_ref, s_row, dst_ref, d_row, sem):
    return pltpu.make_async_copy(src_ref.at[pl.ds(s_row, 1), :], dst_ref.at[pl.ds(d_row, 1), :], sem)


def _moe_scatter_kernel(dest_ref, x_ref, init_ref, o_ref, sem):
    del init_ref
    def issue(r, carry):
        for s in range(2):
            _row_copy(x_ref, r, o_ref, dest_ref[0, 2 * r + s], sem).start()
        return carry
    lax.fori_loop(0, MOE_TR, issue, 0, unroll=8)

    def drain(r, carry):
        for s in range(2):
            _row_copy(x_ref, 0, o_ref, 0, sem).wait()
        return carry
    lax.fori_loop(0, MOE_TR, drain, 0, unroll=8)


def _moe_scatter(x, dest3, n_rows):
    T, D = x.shape
    nt = T // MOE_TR
    return pl.pallas_call(
        _moe_scatter_kernel,
        grid=(nt,),
        in_specs=[pl.BlockSpec((None, 1, 2 * MOE_TR), lambda i: (i, 0, 0), memory_space=pltpu.SMEM),
                  pl.BlockSpec((MOE_TR, D), lambda i: (i, 0)),
                  pl.BlockSpec(memory_space=pl.ANY)],
        out_specs=pl.BlockSpec(memory_space=pl.ANY),
        out_shape=jax.ShapeDtypeStruct((n_rows, D), F32),
        scratch_shapes=[pltpu.SemaphoreType.DMA(())],
        input_output_aliases={2: 0},
        compiler_params=_cparams(("arbitrary",)),
        name="moe_scatter",
    )(dest3, x, jnp.zeros((n_rows, D), F32))


def _moe_combine_kernel(dest_ref, ys_ref, gate_ref, x_ref, g_ref, b_ref, o_ref, buf0, buf1, sem):
    bufs = (buf0, buf1)
    def issue(r, carry):
        for s in range(2):
            _row_copy(ys_ref, dest_ref[0, 2 * r + s], bufs[s], r, sem).start()
        return carry
    lax.fori_loop(0, MOE_TR, issue, 0, unroll=8)

    def drain(r, carry):
        for s in range(2):
            _row_copy(ys_ref, 0, bufs[s], 0, sem).wait()
        return carry
    lax.fori_loop(0, MOE_TR, drain, 0, unroll=8)

    gate = gate_ref[...]
    y = gate[:, 0:1] * buf0[...] + gate[:, 1:2] * buf1[...]
    o_ref[...] = _layer_norm(DN_ALPHA * x_ref[...] + y, g_ref[...], b_ref[...])


def _moe_combine(ys, dest3, gate, x, g, b):
    T, D = x.shape
    nt = T // MOE_TR
    return pl.pallas_call(
        _moe_combine_kernel,
        grid=(nt,),
        in_specs=[pl.BlockSpec((None, 1, 2 * MOE_TR), lambda i: (i, 0, 0), memory_space=pltpu.SMEM),
                  pl.BlockSpec(memory_space=pl.ANY),
                  pl.BlockSpec((MOE_TR, 2), lambda i: (i, 0)),
                  pl.BlockSpec((MOE_TR, D), lambda i: (i, 0)),
                  _full((1, D)), _full((1, D))],
        out_specs=pl.BlockSpec((MOE_TR, D), lambda i: (i, 0)),
        out_shape=jax.ShapeDtypeStruct((T, D), F32),
        scratch_shapes=[pltpu.VMEM((MOE_TR, D), F32), pltpu.VMEM((MOE_TR, D), F32),
                        pltpu.SemaphoreType.DMA(())],
        compiler_params=_cparams(("arbitrary",)),
        name="moe_combine",
    )(dest3, ys, gate, x, g.reshape(1, D), b.reshape(1, D))


def _moe(x, idx, gate, w_gate, w_up, w_down, g, b, tm=1024):
    T, D = x.shape
    pairs = 2 * T
    e = idx.reshape(pairs)
    onehot = (e[:, None] == jnp.arange(N_EXPERTS, dtype=I32)[None, :]).astype(I32)
    csum = jnp.cumsum(onehot, axis=0)
    rank = jnp.sum((csum - onehot) * onehot, axis=1)
    counts = csum[-1]
    padded = ((counts + tm - 1) // tm) * tm
    ends = jnp.cumsum(padded)
    starts = ends - padded
    dest = jnp.sum(onehot * starts[None, :], axis=1) + rank
    n_rows = pairs + N_EXPERTS * tm
    tiles = jnp.arange(n_rows // tm, dtype=I32) * tm
    tile_e = jnp.minimum(jnp.sum((tiles[:, None] >= ends[None, :]).astype(I32), axis=1), N_EXPERTS - 1)
    dest3 = dest.astype(I32).reshape(T // MOE_TR, 1, 2 * MOE_TR)
    xs = _moe_scatter(x, dest3, n_rows)
    ys = _moe_ffn(xs, tile_e.astype(I32), w_gate, w_up, w_down, tm)
    return _moe_combine(ys, dest3, gate, x, g, b)


DIFF_TQ = 512
DIFF_HP = 4


def _qkv_kernel(x_ref, w_ref, q_ref, k_ref, vt_ref):
    xb = x_ref[...].astype(BF16)
    n = q_ref.shape[1]
    hd = n // C_HEADS
    q_ref[...] = (jnp.dot(xb, w_ref[:, 0:n], preferred_element_type=F32)
                  * (C_QK_DIM ** -0.5 * LOG2E)).astype(BF16)
    k_ref[...] = jnp.dot(xb, w_ref[:, n:2 * n], preferred_element_type=F32).astype(BF16)
    for h in range(C_HEADS):
        v = jnp.dot(xb, w_ref[:, 2 * n + h * hd:2 * n + (h + 1) * hd], preferred_element_type=F32)
        vt_ref[0, h] = v.astype(BF16).T


def _qkv(x, w, batch, seq, tm=512):
    T, D = x.shape
    n = w.shape[1] // 3
    hd = n // C_HEADS
    per = seq // tm
    rows = pl.BlockSpec((tm, n), lambda i: (i, 0))
    return pl.pallas_call(
        _qkv_kernel,
        grid=(T // tm,),
        in_specs=[pl.BlockSpec((tm, D), lambda i: (i, 0)), _full(w.shape)],
        out_specs=[rows, rows, pl.BlockSpec((1, C_HEADS, hd, tm), lambda i: (i // per, 0, 0, i % per))],
        out_shape=[jax.ShapeDtypeStruct((T, n), BF16)] * 2
                  + [jax.ShapeDtypeStruct((batch, C_HEADS, hd, seq), BF16)],
        compiler_params=_cparams(("parallel",)),
        name="qkv",
    )(x, w.astype(BF16))


def _diff_kernel(lam_ref, q_ref, k_ref, vt_ref, g_ref, o_ref, acc_ref, *, lambda_init):
    TQ = DIFF_TQ
    hd = 2 * C_QK_DIM
    qi = pl.program_id(2)
    kidx = lax.broadcasted_iota(I32, (TQ, 2 * TQ), 0)
    qcol = lax.broadcasted_iota(I32, (TQ, 2 * TQ), 1)
    causal = kidx <= jnp.where(qcol >= TQ, qcol - TQ, qcol)
    lam_v = lam_ref[...]
    lam = (jnp.exp(jnp.sum(lam_v[0:1] * lam_v[1:2], axis=1, keepdims=True))
           - jnp.exp(jnp.sum(lam_v[2:3] * lam_v[3:4], axis=1, keepdims=True)) + lambda_init)
    first = lax.broadcasted_iota(I32, (TQ, hd), 1) < C_QK_DIM
    q_bd = []
    for i in range(DIFF_HP):
        qh = q_ref[:, i * hd:(i + 1) * hd]
        zero_q = jnp.zeros_like(qh)
        q_bd.append(jnp.concatenate([jnp.where(first, qh, zero_q), jnp.where(first, zero_q, qh)], axis=0))

    def step(c, stats, diagonal):
        keys = pl.ds(pl.multiple_of(c * TQ, TQ), TQ)
        out = []
        for i in range(DIFF_HP):
            vt_aug = _with_ones_rows(vt_ref[0, i, :, keys])
            s_t = lax.dot_general(k_ref[keys, i * hd:(i + 1) * hd], q_bd[i], NT_DIMS,
                                  preferred_element_type=F32)
            if diagonal:
                s_t = jnp.where(causal, s_t, NEG_BIG)
            m_new, alpha, p_t = _softmax_tile(s_t, stats[2 * i])
            pv = jnp.dot(vt_aug, p_t, preferred_element_type=F32)
            acc_ref[i] = alpha * acc_ref[i] + pv[:C_V_DIM]
            out += [m_new, alpha * stats[2 * i + 1] + pv[C_V_DIM:C_V_DIM + 1]]
        return tuple(out)

    acc_ref[...] = jnp.zeros(acc_ref.shape, F32)
    neg = jnp.full((1, 2 * TQ), NEG_BIG, F32)
    zero = jnp.zeros((1, 2 * TQ), F32)
    stats = lax.fori_loop(0, qi, lambda c, st: step(c, st, False), (neg, zero) * DIFF_HP)
    stats = step(qi, stats, True)
    for i in range(DIFF_HP):
        l_i = stats[2 * i + 1]
        o_t = acc_ref[i, :, 0:TQ] / l_i[:, 0:TQ] - lam * (acc_ref[i, :, TQ:2 * TQ] / l_i[:, TQ:2 * TQ])
        o_t = o_t * lax.rsqrt(jnp.mean(o_t * o_t, axis=0, keepdims=True) + RMS_EPS)
        o_ref[:, i * C_V_DIM:(i + 1) * C_V_DIM] = (o_t.T * g_ref[...] * (1.0 - lambda_init)).astype(BF16)


def _diff_attn(q, k, vt, lam4, subln_g, lambda_init, batch, seq):
    T, n = q.shape
    TQ = DIFF_TQ
    nq = seq // TQ
    wide = DIFF_HP * 2 * C_QK_DIM
    return pl.pallas_call(
        functools.partial(_diff_kernel, lambda_init=lambda_init),
        grid=(batch, C_HEADS // DIFF_HP, nq),
        in_specs=[_full((4, C_QK_DIM)),
                  pl.BlockSpec((TQ, wide), lambda b, hg, i: (b * nq + i, hg)),
                  pl.BlockSpec((seq, wide), lambda b, hg, i: (b, hg)),
                  pl.BlockSpec((1, DIFF_HP, C_V_DIM, seq), lambda b, hg, i: (b, hg, 0, 0)),
                  _full((1, C_V_DIM))],
        out_specs=pl.BlockSpec((TQ, DIFF_HP * C_V_DIM), lambda b, hg, i: (b * nq + i, hg)),
        out_shape=jax.ShapeDtypeStruct((T, C_HEADS * C_V_DIM), BF16),
        scratch_shapes=[pltpu.VMEM((DIFF_HP, C_V_DIM, 2 * TQ), F32)],
        compiler_params=_cparams(("parallel", "parallel", "arbitrary")),
        name="diff_attn",
    )(lam4, q, k, vt, subln_g.reshape(1, C_V_DIM))


def _even_layer(x, p, j, batch, seq):
    (aq, qi, bq, bk, bv, br, k, vt, ki, wit, la) = _even_in(
        x, p['ev_w_in'][j], p['ev_a_kv_norm_g'][j], p['ev_a_w_uk'][j], p['ev_a_w_uv'][j],
        p['ev_a_kidx_ln_g'][j], p['ev_a_kidx_ln_b'][j], p['ev_b_w_g2'][j], p['ev_b_b_g'][j])
    a_out = _dsa(aq, qi, wit, ki, k, vt, batch, seq)
    b_out = _gla(bq, bk, bv, la, br, p['ev_b_norm_g'][j], batch, seq)
    w_out = p['ev_w_out'][j]
    na = A_HEADS * A_HEAD_DIM
    x = _proj_ln([a_out, b_out], [w_out[:na], w_out[na:]], x, p['ev_ln1_g'][j], p['ev_ln1_b'][j])
    return _ffn_ln(x, p['ev_ffn_w_gate'][j], p['ev_ffn_w_up'][j], p['ev_ffn_w_down'][j],
                   p['ev_ln2_g'][j], p['ev_ln2_b'][j])


def _odd_layer(x, p, j, layer, batch, seq):
    lambda_init = 0.8 - 0.6 * math.exp(-0.3 * layer)
    q, k, vt = _qkv(x, p['od_w_qkv'][j], batch, seq)
    lam4 = jnp.stack([p['od_lam_q1'][j], p['od_lam_k1'][j], p['od_lam_q2'][j], p['od_lam_k2'][j]])
    o = _diff_attn(q, k, vt, lam4, p['od_subln_g'][j], lambda_init, batch, seq)
    x, idx, gate = _proj_ln([o], [p['od_w_out'][j]], x, p['od_ln1_g'][j], p['od_ln1_b'][j],
                            router=(p['od_router_w'][j], p['od_router_b'][j]))
    return _moe(x, idx, gate, p['od_moe_w_gate'][j], p['od_moe_w_up'][j], p['od_moe_w_down'][j],
                p['od_ln2_g'][j], p['od_ln2_b'][j])


def kernel(x, ev_w_in, ev_a_kv_norm_g, ev_a_w_uk, ev_a_w_uv, ev_a_kidx_ln_g, ev_a_kidx_ln_b,
           ev_b_w_g2, ev_b_b_g, ev_b_norm_g, ev_w_out, ev_ln1_g, ev_ln1_b,
           ev_ffn_w_gate, ev_ffn_w_up, ev_ffn_w_down, ev_ln2_g, ev_ln2_b,
           od_w_qkv, od_lam_q1, od_lam_k1, od_lam_q2, od_lam_k2, od_subln_g, od_w_out,
           od_ln1_g, od_ln1_b, od_router_w, od_router_b,
           od_moe_w_gate, od_moe_w_up, od_moe_w_down, od_ln2_g, od_ln2_b):
    p = dict(locals())
    batch, seq, d = x.shape
    h = x.reshape(batch * seq, d)
    for layer in range(DEPTH):
        j = layer // 2
        if layer % 2 == 0:
            h = _even_layer(h, p, j, batch, seq)
        else:
            h = _odd_layer(h, p, j, layer, batch, seq)
    return h.reshape(batch, seq, d)
```

```python
import functools
import math

import jax
import jax.numpy as jnp
from jax import lax
from jax.experimental import pallas as pl
from jax.experimental.pallas import tpu as pltpu

F32 = jnp.float32
BF16 = jnp.bfloat16
I32 = jnp.int32

DEPTH = 4
DN_ALPHA = (2.0 * DEPTH) ** 0.25
LN_EPS = 1e-5
RMS_EPS = 1e-6

A_HEADS = 8
A_HEAD_DIM = 64
A_KV_LATENT = 128
IDX_HEADS = 8
IDX_DIM = 64
TOPK_MAX = 256

B_HEADS = 4
B_DK = 64
B_DV = 128
B_GATE_RANK = 16
B_GATE_TEMP = 16.0
B_CHUNK = 64

C_HEADS = 8
C_QK_DIM = 64
C_V_DIM = 128

N_EXPERTS = 8

LANES = 128
SUBLANES = 8
WORD_BITS = 32
VMEM_LIMIT = 56 * 1024 * 1024

NEG_BIG = -1e30
INT_MIN = -(2 ** 31)

NT_DIMS = (((1,), (1,)), ((), ()))
TN_DIMS = (((0,), (0,)), ((), ()))


def _cparams(sem):
    return pltpu.CompilerParams(dimension_semantics=sem, vmem_limit_bytes=VMEM_LIMIT)


def _layer_norm(y, g, b):
    mu = jnp.mean(y, axis=-1, keepdims=True)
    d = y - mu
    var = jnp.mean(d * d, axis=-1, keepdims=True)
    return d * lax.rsqrt(var + LN_EPS) * g + b


def _rms_norm(y, g):
    return y * lax.rsqrt(jnp.mean(y * y, axis=-1, keepdims=True) + RMS_EPS) * g


def _sigmoid(z):
    return 1.0 / (1.0 + jnp.exp(-z))


def _full(shape):
    return pl.BlockSpec(shape, lambda *_: (0,) * len(shape))


LOG2E = math.log2(math.e)
ONES_ROWS = 16


def _softmax_tile(s_t, m_old):
    m_new = jnp.maximum(m_old, jnp.max(s_t, axis=0, keepdims=True))
    alpha = jnp.exp2(m_old - m_new)
    p_t = jnp.exp2((s_t - m_new).astype(BF16))
    return m_new, alpha, p_t


def _with_ones_rows(vt):
    return jnp.concatenate([vt, jnp.ones((ONES_ROWS, vt.shape[1]), BF16)], axis=0)


def _even_in_kernel(x_ref, wbig_ref, wsm_ref, wwit_ref, kvg_ref, wuk_ref, wuvt_ref, lng_ref, lnb_ref,
                    wg2_ref, bg2_ref,
                    aq_ref, qi_ref, bq_ref, bk_ref, bv_ref, br_ref, k_ref, vt_ref, ki_ref, wit_ref, la_ref):
    xb = x_ref[...].astype(BF16)

    def proj(lo, hi):
        return jnp.dot(xb, wbig_ref[:, lo:hi], preferred_element_type=F32)

    aq = (proj(0, 512) * (A_HEAD_DIM ** -0.5 * LOG2E)).astype(BF16)
    qi = (proj(512, 1024) * IDX_DIM ** -0.5).astype(BF16)
    for h in range(A_HEADS):
        aq_ref[h] = aq[:, h * A_HEAD_DIM:(h + 1) * A_HEAD_DIM]
        qi_ref[h] = qi[:, h * IDX_DIM:(h + 1) * IDX_DIM]
    bq_ref[...] = (proj(1024, 1280) * B_DK ** -0.5).astype(BF16)
    bk_ref[...] = proj(1280, 1536).astype(BF16)
    bv_ref[...] = proj(1536, 2048).astype(BF16)
    br_ref[...] = proj(2048, 2560).astype(BF16)

    sm = jnp.dot(xb, wsm_ref[...], preferred_element_type=F32)
    ckv = sm[:, 0:128]
    aki = sm[:, 128:192]
    bg = sm[:, 192:208]

    c = _rms_norm(ckv, kvg_ref[...]).astype(BF16)
    k_ref[...] = jnp.dot(c, wuk_ref[...], preferred_element_type=F32).astype(BF16)
    vt_ref[...] = lax.dot_general(wuvt_ref[...], c, NT_DIMS, preferred_element_type=F32).astype(BF16)
    ki_ref[...] = _layer_norm(aki, lng_ref[...], lnb_ref[...]).astype(BF16)
    wit_ref[...] = lax.dot_general(wwit_ref[...], xb, NT_DIMS, preferred_element_type=F32) * IDX_HEADS ** -0.5
    z = jnp.dot(bg, wg2_ref[...], preferred_element_type=F32, precision=lax.Precision.HIGHEST) + bg2_ref[...]
    la_ref[...] = (jnp.minimum(z, 0.0) - jnp.log1p(jnp.exp(-jnp.abs(z)))) * (1.0 / B_GATE_TEMP)


def _even_in(x, w_in, kv_g, w_uk, w_uv, ln_g, ln_b, w_g2, b_g, tm=512):
    T, D = x.shape
    o = [0, 512, 640, 1152, 1216, 1224, 1480, 1736, 2248, 2264, 2776]
    col = lambda i: w_in[:, o[i]:o[i + 1]]
    wbig = jnp.concatenate([col(0), col(2), col(5), col(6), col(7), col(9)], axis=1).astype(BF16)
    wsm = jnp.concatenate([col(1), col(3), col(8), jnp.zeros((D, 48), F32)], axis=1).astype(BF16)
    wwit = col(4).T.astype(BF16)
    rows = lambda n, dt: (pl.BlockSpec((tm, n), lambda i: (i, 0)), jax.ShapeDtypeStruct((T, n), dt))
    cols = lambda n, dt: (pl.BlockSpec((n, tm), lambda i: (0, i)), jax.ShapeDtypeStruct((n, T), dt))
    heads = (pl.BlockSpec((A_HEADS, tm, A_HEAD_DIM), lambda i: (0, i, 0)),
             jax.ShapeDtypeStruct((A_HEADS, T, A_HEAD_DIM), BF16))
    outs = [heads, heads, rows(256, BF16), rows(256, BF16), rows(512, BF16), rows(512, BF16),
            rows(64, BF16), cols(64, BF16), rows(64, BF16), cols(8, F32), rows(256, F32)]
    return pl.pallas_call(
        _even_in_kernel,
        grid=(T // tm,),
        in_specs=[pl.BlockSpec((tm, D), lambda i: (i, 0)),
                  _full((D, 2560)), _full((D, 256)), _full((8, D)), _full((1, 128)), _full((128, 64)),
                  _full((64, 128)), _full((1, 64)), _full((1, 64)), _full((16, 256)), _full((1, 256))],
        out_specs=[s for s, _ in outs],
        out_shape=[s for _, s in outs],
        compiler_params=_cparams(("parallel",)),
        name="even_in",
    )(x, wbig, wsm, wwit, kv_g.reshape(1, 128), w_uk.astype(BF16), w_uv.T.astype(BF16),
      ln_g.reshape(1, 64), ln_b.reshape(1, 64), w_g2, b_g.reshape(1, 256))


DSA_TQ = 256
DSA_KC = 256


def _dsa_kernel(aq_ref, qi_ref, wit_ref, ki_ref, k_ref, vt_ref, o_ref,
                keys_ref, planes_ref, midx_ref, acc_ref, out_ref, *, seq, n_sel):
    TQ, KC = DSA_TQ, DSA_KC
    NQ = A_HEADS * TQ
    q_all = aq_ref[...].reshape(NQ, A_HEAD_DIM)
    qi_all = qi_ref[...].reshape(IDX_HEADS * TQ, IDX_DIM)
    qb = pl.program_id(1)
    n_kc = (qb * TQ + TQ + KC - 1) // KC
    kpos0 = lax.broadcasted_iota(I32, (KC, TQ), 0)
    qpos = qb * TQ + lax.broadcasted_iota(I32, (KC, TQ), 1)
    w_t = wit_ref[...]

    def idx_body(c, carry):
        off = pl.multiple_of(c * KC, KC)
        kic = ki_ref[pl.ds(off, KC), :]
        rel = lax.dot_general(kic, qi_all, NT_DIMS, preferred_element_type=F32)
        score = jnp.zeros((KC, TQ), F32)
        for h in range(IDX_HEADS):
            score = score + w_t[h:h + 1, :] * jnp.maximum(rel[:, h * TQ:(h + 1) * TQ], 0.0)
        score = jnp.where(score == 0.0, 0.0, score)
        bits = lax.bitcast_convert_type(score, I32)
        key = jnp.where(bits < 0, bits ^ jnp.int32(0x7FFFFFFF), bits)
        key = jnp.where(off + kpos0 <= qpos, key, jnp.int32(INT_MIN))
        keys_ref[c] = key
        x = [key[v * SUBLANES:(v + 1) * SUBLANES, :] ^ jnp.int32(INT_MIN) for v in range(WORD_BITS)]
        for j, mask in ((16, 0x0000FFFF), (8, 0x00FF00FF), (4, 0x0F0F0F0F), (2, 0x33333333), (1, 0x55555555)):
            for r in range(WORD_BITS):
                if r & j == 0:
                    tmp = (x[r] ^ lax.shift_right_logical(x[r + j], jnp.int32(j))) & jnp.int32(mask)
                    x[r] = x[r] ^ tmp
                    x[r + j] = x[r + j] ^ lax.shift_left(tmp, jnp.int32(j))
        for i in range(WORD_BITS):
            planes_ref[c, i] = x[i]
        return carry

    lax.fori_loop(0, n_kc, idx_body, 0)
    n_chunks = seq // KC

    def clear_body(c, carry):
        planes_ref[c] = jnp.zeros(planes_ref.shape[1:], I32)
        return carry

    lax.fori_loop(n_kc, n_chunks, clear_body, 0)

    in_play0 = tuple(jnp.where(c < n_kc, jnp.int32(-1), jnp.int32(0)) * jnp.ones((SUBLANES, TQ), I32)
                     for c in range(n_chunks))

    def set_bits(words):
        per_word = functools.reduce(lambda a, b: a + b, [lax.population_count(w) for w in words])
        return jnp.sum(per_word.astype(F32), axis=0, keepdims=True)

    def bit_body(i, carry):
        t_u, need, in_play = carry
        ones = [in_play[c] & planes_ref[c, i] for c in range(n_chunks)]
        cnt = set_bits(ones)
        take = cnt >= need
        in_play = tuple(jnp.where(take, o, p & ~o) for o, p in zip(ones, in_play))
        t_u = jnp.where(take, t_u | lax.shift_left(jnp.int32(1), jnp.int32(31) - i), t_u)
        return t_u, jnp.where(take, need, need - cnt), in_play

    t_u, need, eq_words = lax.fori_loop(
        0, WORD_BITS, bit_body,
        (jnp.zeros((1, TQ), I32), jnp.full((1, TQ), float(n_sel), F32), in_play0))
    t = t_u ^ jnp.int32(INT_MIN)
    c_eq = set_bits(list(eq_words))
    has_row = t != jnp.int32(INT_MIN)
    midx_ref[...] = jnp.broadcast_to(jnp.where(has_row, jnp.int32(seq), jnp.int32(-1)), midx_ref.shape)
    tie = jnp.where(has_row & (c_eq > need), 1.0, 0.0)

    def count(pred):
        def body(c, acc):
            hit = jnp.where(pred(keys_ref[c], c * KC + kpos0), 1.0, 0.0)
            return acc + jnp.sum(hit.reshape(KC // SUBLANES, SUBLANES, TQ), axis=0)
        acc = lax.fori_loop(0, n_kc, body, jnp.zeros((SUBLANES, TQ), F32))
        return jnp.sum(acc, axis=0, keepdims=True)

    @pl.when(jnp.max(tie) > 0.0)
    def _():
        def jbit(i, m):
            cand = m + lax.shift_left(jnp.int32(1), jnp.int32(11) - i)
            cnt = count(lambda kc, pos: (kc == t) & (pos < cand))
            return jnp.where(cnt < need, cand, m)
        m = lax.fori_loop(0, 12, jbit, jnp.zeros((1, TQ), I32))
        midx_ref[...] = jnp.broadcast_to(jnp.where(has_row, m, jnp.int32(-1)), midx_ref.shape)

    midx = midx_ref[0:1, :]

    acc_ref[...] = jnp.zeros(acc_ref.shape, F32)

    def att_body(c, stats):
        m_old, l_old = stats
        off = pl.multiple_of(c * KC, KC)
        kc = keys_ref[c]
        sel = (kc > t) | ((kc == t) & (off + kpos0 <= midx))
        bias = jnp.where(sel, 0.0, NEG_BIG)
        kk = k_ref[pl.ds(off, KC), :]
        vt_aug = _with_ones_rows(vt_ref[:, pl.ds(off, KC)])
        s_t = lax.dot_general(kk, q_all, NT_DIMS, preferred_element_type=F32)
        s_t = s_t + jnp.concatenate([bias] * A_HEADS, axis=1)
        m_new, alpha, p_t = _softmax_tile(s_t, m_old)
        pv = jnp.dot(vt_aug, p_t, preferred_element_type=F32)
        acc_ref[...] = alpha * acc_ref[...] + pv[:A_HEAD_DIM]
        return m_new, alpha * l_old + pv[A_HEAD_DIM:A_HEAD_DIM + 1]

    _, l_fin = lax.fori_loop(0, n_kc, att_body,
                             (jnp.full((1, NQ), NEG_BIG, F32), jnp.zeros((1, NQ), F32)))
    o_t = acc_ref[...] / l_fin
    for h in range(A_HEADS):
        out_ref[h * A_HEAD_DIM:(h + 1) * A_HEAD_DIM, :] = o_t[:, h * TQ:(h + 1) * TQ]
    o_ref[...] = out_ref[...].T.astype(BF16)


def _dsa(aq, qi, wit, ki, k, vt, batch, seq):
    T = aq.shape[1]
    TQ, KC = DSA_TQ, DSA_KC
    assert KC == WORD_BITS * SUBLANES
    nq = seq // TQ
    n_sel = min(TOPK_MAX, seq // 4)
    heads = pl.BlockSpec((A_HEADS, TQ, A_HEAD_DIM), lambda b, q: (0, b * nq + q, 0))
    whole = lambda n: pl.BlockSpec((seq, n), lambda b, q: (b, 0))
    return pl.pallas_call(
        functools.partial(_dsa_kernel, seq=seq, n_sel=n_sel),
        grid=(batch, nq),
        in_specs=[heads, heads,
                  pl.BlockSpec((IDX_HEADS, TQ), lambda b, q: (0, b * nq + q)),
                  whole(64), whole(64),
                  pl.BlockSpec((A_HEAD_DIM, seq), lambda b, q: (0, b))],
        out_specs=pl.BlockSpec((TQ, A_HEADS * A_HEAD_DIM), lambda b, q: (b * nq + q, 0)),
        out_shape=jax.ShapeDtypeStruct((T, A_HEADS * A_HEAD_DIM), BF16),
        scratch_shapes=[pltpu.VMEM((seq // KC, KC, TQ), I32),
                        pltpu.VMEM((seq // KC, WORD_BITS, SUBLANES, TQ), I32),
                        pltpu.VMEM((SUBLANES, TQ), I32),
                        pltpu.VMEM((A_HEAD_DIM, A_HEADS * TQ), F32),
                        pltpu.VMEM((A_HEADS * A_HEAD_DIM, TQ), F32)],
        compiler_params=_cparams(("parallel", "arbitrary")),
        name="dsa",
    )(aq, qi, wit, ki, k, vt)


GLA_ROWS = 2


def _gla_kernel(q_ref, k_ref, v_ref, la_ref, r_ref, ng_ref, o_ref, st_ref, *, seq):
    C = B_CHUNK
    half = C // 2
    row = lax.broadcasted_iota(I32, (C, C), 0)
    colm = lax.broadcasted_iota(I32, (C, C), 1)
    tril = row >= colm
    trilf = jnp.where(tril, 1.0, 0.0)
    st_ref[...] = jnp.zeros(st_ref.shape, F32)

    def chunk(rows, n):
        la = la_ref[rows, :]
        b = jnp.dot(trilf, la, preferred_element_type=F32, precision=lax.Precision.HIGHEST)
        b_mid = b[half - 1:half, :]
        b_last = b[C - 1:C, :]
        q = q_ref[rows, :].astype(F32)
        k = k_ref[rows, :].astype(F32)
        q_in = (q * jnp.exp(b - b_mid)).astype(BF16)
        k_in = (k * jnp.exp(b_mid - b)).astype(BF16)
        q_st = (q * jnp.exp(b)).astype(BF16)
        k_st = (k * jnp.exp(b_last - b)).astype(BF16)
        e_last = jnp.exp(b_last)
        for h in range(B_HEADS):
            ks = slice(h * B_DK, (h + 1) * B_DK)
            vs = slice(h * B_DV, (h + 1) * B_DV)
            a = lax.dot_general(q_in[:, ks], k_in[:, ks], NT_DIMS, preferred_element_type=F32)
            a = jnp.where(tril, a, 0.0).astype(BF16)
            vh = v_ref[rows, vs]
            st = st_ref[n, h]
            o = (jnp.dot(a, vh, preferred_element_type=F32)
                 + lax.dot_general(q_st[:, ks], st.astype(BF16), NT_DIMS, preferred_element_type=F32))
            st_ref[n, h] = st * e_last[:, ks] + lax.dot_general(vh, k_st[:, ks], TN_DIMS,
                                                               preferred_element_type=F32)
            r = r_ref[rows, vs].astype(F32)
            o_ref[rows, vs] = (_rms_norm(o, ng_ref[h:h + 1, :]) * (r * _sigmoid(r))).astype(BF16)

    def body(c, carry):
        for n in range(GLA_ROWS):
            chunk(pl.ds(pl.multiple_of(n * seq + c * C, C), C), n)
        return carry

    lax.fori_loop(0, seq // C, body, 0)


def _gla(bq, bk, bv, la, br, norm_g, batch, seq):
    T = bq.shape[0]
    blk = lambda n: pl.BlockSpec((GLA_ROWS * seq, n), lambda b: (b, 0))
    return pl.pallas_call(
        functools.partial(_gla_kernel, seq=seq),
        grid=(batch // GLA_ROWS,),
        in_specs=[blk(256), blk(256), blk(512), blk(256), blk(512), _full((B_HEADS, B_DV))],
        out_specs=blk(512),
        out_shape=jax.ShapeDtypeStruct((T, 512), BF16),
        scratch_shapes=[pltpu.VMEM((GLA_ROWS, B_HEADS, B_DV, B_DK), F32)],
        compiler_params=_cparams(("parallel",)),
        name="gla",
    )(bq, bk, bv, la, br, norm_g)


def _proj_ln_kernel(*refs, n_in, router):
    acts = refs[:n_in]
    ws = refs[n_in:2 * n_in]
    x_ref, g_ref, b_ref = refs[2 * n_in:2 * n_in + 3]
    rest = refs[2 * n_in + 3:]
    h = jnp.dot(acts[0][...], ws[0][...], preferred_element_type=F32)
    for a, w in zip(acts[1:], ws[1:]):
        h = h + jnp.dot(a[...], w[...], preferred_element_type=F32)
    y = _layer_norm(DN_ALPHA * x_ref[...] + h, g_ref[...], b_ref[...])
    if not router:
        rest[0][...] = y
        return
    wr_ref, br_ref, o_ref, idx_ref, gate_ref = rest
    o_ref[...] = y
    y_hi = y.astype(BF16)
    y_lo = (y - y_hi.astype(F32)).astype(BF16)
    hi = jnp.dot(y_hi, wr_ref[...], preferred_element_type=F32)
    lo = jnp.dot(y_lo, wr_ref[:, 0:N_EXPERTS], preferred_element_type=F32)
    logits = hi[:, 0:N_EXPERTS] + hi[:, N_EXPERTS:2 * N_EXPERTS] + lo + br_ref[...]
    tm = logits.shape[0]
    eid = lax.broadcasted_iota(I32, (tm, N_EXPERTS), 1)
    m1 = jnp.max(logits, axis=1, keepdims=True)
    i1 = jnp.min(jnp.where(logits == m1, eid, N_EXPERTS), axis=1, keepdims=True)
    rest_l = jnp.where(eid == i1, -jnp.inf, logits)
    m2 = jnp.max(rest_l, axis=1, keepdims=True)
    i2 = jnp.min(jnp.where(rest_l == m2, eid, N_EXPERTS), axis=1, keepdims=True)
    e2 = jnp.exp(m2 - m1)
    g1 = 1.0 / (1.0 + e2)
    g2 = e2 / (1.0 + e2)
    two = lax.broadcasted_iota(I32, (tm, 2), 1)
    idx_ref[...] = jnp.where(two == 0, i1, i2)
    gate_ref[...] = jnp.where(two == 0, g1, g2)


def _proj_ln(acts, ws, x, g, b, router=None, tm=512):
    T, D = x.shape
    n_in = len(acts)
    act_specs = [pl.BlockSpec((tm, a.shape[1]), lambda i: (i, 0)) for a in acts]
    in_specs = (act_specs + [_full(w.shape) for w in ws]
                + [pl.BlockSpec((tm, D), lambda i: (i, 0)), _full((1, D)), _full((1, D))])
    args = list(acts) + [w.astype(BF16) for w in ws] + [x, g.reshape(1, D), b.reshape(1, D)]
    out_specs = [pl.BlockSpec((tm, D), lambda i: (i, 0))]
    out_shape = [jax.ShapeDtypeStruct((T, D), F32)]
    if router is not None:
        w_r, b_r = router
        w_hi = w_r.astype(BF16)
        w_lo = (w_r - w_hi.astype(F32)).astype(BF16)
        in_specs += [_full((D, 2 * N_EXPERTS)), _full((1, N_EXPERTS))]
        args += [jnp.concatenate([w_hi, w_lo], axis=1), b_r.reshape(1, N_EXPERTS)]
        out_specs += [pl.BlockSpec((tm, 2), lambda i: (i, 0))] * 2
        out_shape += [jax.ShapeDtypeStruct((T, 2), I32), jax.ShapeDtypeStruct((T, 2), F32)]
    res = pl.pallas_call(
        functools.partial(_proj_ln_kernel, n_in=n_in, router=router is not None),
        grid=(T // tm,),
        in_specs=in_specs, out_specs=out_specs, out_shape=out_shape,
        compiler_params=_cparams(("parallel",)),
        name="proj_ln_router" if router is not None else "proj_ln",
    )(*args)
    return res if router is not None else res[0]


FFN_CHUNK = 512


def _swiglu_rows(x_ref, wg_ref, wu_ref, wd_ref, hid_ref):
    xb = x_ref[...].astype(BF16)
    n_hidden = wg_ref.shape[-1]
    for lo in range(0, n_hidden, FFN_CHUNK):
        cs = slice(lo, min(lo + FFN_CHUNK, n_hidden))
        gt = jnp.dot(xb, wg_ref[:, cs], preferred_element_type=F32)
        up = jnp.dot(xb, wu_ref[:, cs], preferred_element_type=F32)
        hid_ref[:, cs] = (gt * _sigmoid(gt) * up).astype(BF16)
    return jnp.dot(hid_ref[...], wd_ref[...], preferred_element_type=F32)


def _ffn_ln_kernel(x_ref, wg_ref, wu_ref, wd_ref, g_ref, b_ref, o_ref, hid_ref):
    h = _swiglu_rows(x_ref, wg_ref, wu_ref, wd_ref, hid_ref)
    o_ref[...] = _layer_norm(DN_ALPHA * x_ref[...] + h, g_ref[...], b_ref[...])


def _resident(shape, index_map):
    return pl.BlockSpec(shape, index_map, pipeline_mode=pl.Buffered(1))


def _ffn_ln(x, w_gate, w_up, w_down, j, g, b, tm=512):
    T, D = x.shape
    F = w_gate.shape[2]
    return pl.pallas_call(
        _ffn_ln_kernel,
        grid=(T // tm,),
        in_specs=[pl.BlockSpec((tm, D), lambda i: (i, 0)),
                  _resident((None, D, F), lambda i: (j, 0, 0)),
                  _resident((None, D, F), lambda i: (j, 0, 0)),
                  _resident((None, F, D), lambda i: (j, 0, 0)),
                  _full((1, D)), _full((1, D))],
        out_specs=pl.BlockSpec((tm, D), lambda i: (i, 0)),
        out_shape=jax.ShapeDtypeStruct((T, D), F32),
        scratch_shapes=[pltpu.VMEM((tm, F), BF16)],
        compiler_params=_cparams(("parallel",)),
        name="ffn_ln",
    )(x, w_gate.astype(BF16), w_up.astype(BF16), w_down.astype(BF16), g.reshape(1, D), b.reshape(1, D))


def _moe_ffn_kernel(te_ref, x_ref, wg_ref, wu_ref, wd_ref, o_ref, hid_ref):
    used = te_ref[pl.num_programs(0)]

    @pl.when(pl.program_id(0) < used)
    def _():
        o_ref[...] = _swiglu_rows(x_ref, wg_ref, wu_ref, wd_ref, hid_ref)

    @pl.when(pl.program_id(0) >= used)
    def _():
        o_ref[...] = jnp.zeros(o_ref.shape, F32)


def _moe_ffn(xs, tile_e, w_gate, w_up, w_down, j, tm):
    R, D = xs.shape
    F = w_gate.shape[3]
    grid_spec = pltpu.PrefetchScalarGridSpec(
        num_scalar_prefetch=1,
        grid=(R // tm,),
        in_specs=[pl.BlockSpec((tm, D), lambda i, te: (i, 0)),
                  _resident((None, None, D, F), lambda i, te: (j, te[i], 0, 0)),
                  _resident((None, None, D, F), lambda i, te: (j, te[i], 0, 0)),
                  _resident((None, None, F, D), lambda i, te: (j, te[i], 0, 0))],
        out_specs=pl.BlockSpec((tm, D), lambda i, te: (i, 0)),
        scratch_shapes=[pltpu.VMEM((tm, F), BF16)],
    )
    return pl.pallas_call(
        _moe_ffn_kernel,
        grid_spec=grid_spec,
        out_shape=jax.ShapeDtypeStruct((R, D), F32),
        compiler_params=_cparams(("arbitrary",)),
        name="moe_ffn",
    )(tile_e, xs, w_gate.astype(BF16), w_up.astype(BF16), w_down.astype(BF16))


MOE_TR = 256


def _row_copy(src_ref, s_row, dst_ref, d_row, sem):
    return pltpu.make_async_copy(src_ref.at[pl.ds(s_row, 1), :], dst_ref.at[pl.ds(d_row, 1), :], sem)


def _moe_scatter_kernel(dest_ref, x_ref, init_ref, o_ref, sem):
    del init_ref
    def issue(r, carry):
        for s in range(2):
            _row_copy(x_ref, r, o_ref, dest_ref[0, 2 * r + s], sem).start()
        return carry
    lax.fori_loop(0, MOE_TR, issue, 0, unroll=8)

    def drain(r, carry):
        for s in range(2):
            _row_copy(x_ref, 0, o_ref, 0, sem).wait()
        return carry
    lax.fori_loop(0, MOE_TR, drain, 0, unroll=8)


def _moe_scatter(x, dest3, n_rows):
    T, D = x.shape
    nt = T // MOE_TR
    return pl.pallas_call(
        _moe_scatter_kernel,
        grid=(nt,),
        in_specs=[pl.BlockSpec((None, 1, 2 * MOE_TR), lambda i: (i, 0, 0), memory_space=pltpu.SMEM),
                  pl.BlockSpec((MOE_TR, D), lambda i: (i, 0)),
                  pl.BlockSpec(memory_space=pl.ANY)],
        out_specs=pl.BlockSpec(memory_space=pl.ANY),
        out_shape=jax.ShapeDtypeStruct((n_rows, D), F32),
        scratch_shapes=[pltpu.SemaphoreType.DMA(())],
        input_output_aliases={2: 0},
        compiler_params=_cparams(("arbitrary",)),
        name="moe_scatter",
    )(dest3, x, jnp.zeros((n_rows, D), F32))


def _moe_combine_kernel(dest_ref, ys_ref, gate_ref, x_ref, g_ref, b_ref, o_ref, buf0, buf1, sem):
    bufs = (buf0, buf1)
    def issue(r, carry):
        for s in range(2):
            _row_copy(ys_ref, dest_ref[0, 2 * r + s], bufs[s], r, sem).start()
        return carry
    lax.fori_loop(0, MOE_TR, issue, 0, unroll=8)

    def drain(r, carry):
        for s in range(2):
            _row_copy(ys_ref, 0, bufs[s], 0, sem).wait()
        return carry
    lax.fori_loop(0, MOE_TR, drain, 0, unroll=8)

    gate = gate_ref[...]
    y = gate[:, 0:1] * buf0[...] + gate[:, 1:2] * buf1[...]
    o_ref[...] = _layer_norm(DN_ALPHA * x_ref[...] + y, g_ref[...], b_ref[...])


def _moe_combine(ys, dest3, gate, x, g, b):
    T, D = x.shape
    nt = T // MOE_TR
    return pl.pallas_call(
        _moe_combine_kernel,
        grid=(nt,),
        in_specs=[pl.BlockSpec((None, 1, 2 * MOE_TR), lambda i: (i, 0, 0), memory_space=pltpu.SMEM),
                  pl.BlockSpec(memory_space=pl.ANY),
                  pl.BlockSpec((MOE_TR, 2), lambda i: (i, 0)),
                  pl.BlockSpec((MOE_TR, D), lambda i: (i, 0)),
                  _full((1, D)), _full((1, D))],
        out_specs=pl.BlockSpec((MOE_TR, D), lambda i: (i, 0)),
        out_shape=jax.ShapeDtypeStruct((T, D), F32),
        scratch_shapes=[pltpu.VMEM((MOE_TR, D), F32), pltpu.VMEM((MOE_TR, D), F32),
                        pltpu.SemaphoreType.DMA(())],
        compiler_params=_cparams(("arbitrary",)),
        name="moe_combine",
    )(dest3, ys, gate, x, g.reshape(1, D), b.reshape(1, D))


def _moe(x, idx, gate, w_gate, w_up, w_down, j, g, b, tm=1024):
    T, D = x.shape
    pairs = 2 * T
    e = idx.reshape(pairs)
    onehot = (e[:, None] == jnp.arange(N_EXPERTS, dtype=I32)[None, :]).astype(I32)
    csum = jnp.cumsum(onehot, axis=0)
    rank = jnp.sum((csum - onehot) * onehot, axis=1)
    counts = csum[-1]
    padded = ((counts + tm - 1) // tm) * tm
    ends = jnp.cumsum(padded)
    starts = ends - padded
    dest = jnp.sum(onehot * starts[None, :], axis=1) + rank
    n_rows = pairs + N_EXPERTS * tm
    tiles = jnp.arange(n_rows // tm, dtype=I32) * tm
    tile_e = jnp.minimum(jnp.sum((tiles[:, None] >= ends[None, :]).astype(I32), axis=1), N_EXPERTS - 1)
    dest3 = dest.astype(I32).reshape(T // MOE_TR, 1, 2 * MOE_TR)
    xs = _moe_scatter(x, dest3, n_rows)
    n_used = (ends[-1] // tm).astype(I32)
    ys = _moe_ffn(xs, jnp.concatenate([tile_e.astype(I32), n_used[None]]), w_gate, w_up, w_down, j, tm)
    return _moe_combine(ys, dest3, gate, x, g, b)


DIFF_TQ = 512
DIFF_HP = 4


def _qkv_kernel(x_ref, w_ref, q_ref, k_ref, vt_ref):
    xb = x_ref[...].astype(BF16)
    n = q_ref.shape[1]
    hd = n // C_HEADS
    q_ref[...] = (jnp.dot(xb, w_ref[:, 0:n], preferred_element_type=F32)
                  * (C_QK_DIM ** -0.5 * LOG2E)).astype(BF16)
    k_ref[...] = jnp.dot(xb, w_ref[:, n:2 * n], preferred_element_type=F32).astype(BF16)
    for h in range(C_HEADS):
        v = jnp.dot(xb, w_ref[:, 2 * n + h * hd:2 * n + (h + 1) * hd], preferred_element_type=F32)
        vt_ref[0, h] = v.astype(BF16).T


def _qkv(x, w, batch, seq, tm=512):
    T, D = x.shape
    n = w.shape[1] // 3
    hd = n // C_HEADS
    per = seq // tm
    rows = pl.BlockSpec((tm, n), lambda i: (i, 0))
    return pl.pallas_call(
        _qkv_kernel,
        grid=(T // tm,),
        in_specs=[pl.BlockSpec((tm, D), lambda i: (i, 0)), _full(w.shape)],
        out_specs=[rows, rows, pl.BlockSpec((1, C_HEADS, hd, tm), lambda i: (i // per, 0, 0, i % per))],
        out_shape=[jax.ShapeDtypeStruct((T, n), BF16)] * 2
                  + [jax.ShapeDtypeStruct((batch, C_HEADS, hd, seq), BF16)],
        compiler_params=_cparams(("parallel",)),
        name="qkv",
    )(x, w.astype(BF16))


def _diff_kernel(lam_ref, q_ref, k_ref, vt_ref, g_ref, o_ref, acc_ref, *, lambda_init):
    TQ = DIFF_TQ
    hd = 2 * C_QK_DIM
    qi = pl.program_id(2)
    kidx = lax.broadcasted_iota(I32, (TQ, 2 * TQ), 0)
    qcol = lax.broadcasted_iota(I32, (TQ, 2 * TQ), 1)
    causal = kidx <= jnp.where(qcol >= TQ, qcol - TQ, qcol)
    lam_v = lam_ref[...]
    lam = (jnp.exp(jnp.sum(lam_v[0:1] * lam_v[1:2], axis=1, keepdims=True))
           - jnp.exp(jnp.sum(lam_v[2:3] * lam_v[3:4], axis=1, keepdims=True)) + lambda_init)
    first = lax.broadcasted_iota(I32, (TQ, hd), 1) < C_QK_DIM
    q_bd = []
    for i in range(DIFF_HP):
        qh = q_ref[:, i * hd:(i + 1) * hd]
        zero_q = jnp.zeros_like(qh)
        q_bd.append(jnp.concatenate([jnp.where(first, qh, zero_q), jnp.where(first, zero_q, qh)], axis=0))

    def step(c, stats, diagonal):
        keys = pl.ds(pl.multiple_of(c * TQ, TQ), TQ)
        out = []
        for i in range(DIFF_HP):
            vt_aug = _with_ones_rows(vt_ref[0, i, :, keys])
            s_t = lax.dot_general(k_ref[keys, i * hd:(i + 1) * hd], q_bd[i], NT_DIMS,
                                  preferred_element_type=F32)
            if diagonal:
                s_t = jnp.where(causal, s_t, NEG_BIG)
            m_new, alpha, p_t = _softmax_tile(s_t, stats[2 * i])
            pv = jnp.dot(vt_aug, p_t, preferred_element_type=F32)
            acc_ref[i] = alpha * acc_ref[i] + pv[:C_V_DIM]
            out += [m_new, alpha * stats[2 * i + 1] + pv[C_V_DIM:C_V_DIM + 1]]
        return tuple(out)

    acc_ref[...] = jnp.zeros(acc_ref.shape, F32)
    neg = jnp.full((1, 2 * TQ), NEG_BIG, F32)
    zero = jnp.zeros((1, 2 * TQ), F32)
    stats = lax.fori_loop(0, qi, lambda c, st: step(c, st, False), (neg, zero) * DIFF_HP)
    stats = step(qi, stats, True)
    for i in range(DIFF_HP):
        l_i = stats[2 * i + 1]
        o_t = acc_ref[i, :, 0:TQ] / l_i[:, 0:TQ] - lam * (acc_ref[i, :, TQ:2 * TQ] / l_i[:, TQ:2 * TQ])
        o_t = o_t * lax.rsqrt(jnp.mean(o_t * o_t, axis=0, keepdims=True) + RMS_EPS)
        o_ref[:, i * C_V_DIM:(i + 1) * C_V_DIM] = (o_t.T * g_ref[...] * (1.0 - lambda_init)).astype(BF16)


def _diff_attn(q, k, vt, lam4, subln_g, lambda_init, batch, seq):
    T, n = q.shape
    TQ = DIFF_TQ
    nq = seq // TQ
    wide = DIFF_HP * 2 * C_QK_DIM
    return pl.pallas_call(
        functools.partial(_diff_kernel, lambda_init=lambda_init),
        grid=(batch, C_HEADS // DIFF_HP, nq),
        in_specs=[_full((4, C_QK_DIM)),
                  pl.BlockSpec((TQ, wide), lambda b, hg, i: (b * nq + i, hg)),
                  pl.BlockSpec((seq, wide), lambda b, hg, i: (b, hg)),
                  pl.BlockSpec((1, DIFF_HP, C_V_DIM, seq), lambda b, hg, i: (b, hg, 0, 0)),
                  _full((1, C_V_DIM))],
        out_specs=pl.BlockSpec((TQ, DIFF_HP * C_V_DIM), lambda b, hg, i: (b * nq + i, hg)),
        out_shape=jax.ShapeDtypeStruct((T, C_HEADS * C_V_DIM), BF16),
        scratch_shapes=[pltpu.VMEM((DIFF_HP, C_V_DIM, 2 * TQ), F32)],
        compiler_params=_cparams(("parallel", "parallel", "arbitrary")),
        name="diff_attn",
    )(lam4, q, k, vt, subln_g.reshape(1, C_V_DIM))


def _even_layer(x, p, j, batch, seq):
    (aq, qi, bq, bk, bv, br, k, vt, ki, wit, la) = _even_in(
        x, p['ev_w_in'][j], p['ev_a_kv_norm_g'][j], p['ev_a_w_uk'][j], p['ev_a_w_uv'][j],
        p['ev_a_kidx_ln_g'][j], p['ev_a_kidx_ln_b'][j], p['ev_b_w_g2'][j], p['ev_b_b_g'][j])
    a_out = _dsa(aq, qi, wit, ki, k, vt, batch, seq)
    b_out = _gla(bq, bk, bv, la, br, p['ev_b_norm_g'][j], batch, seq)
    w_out = p['ev_w_out'][j]
    na = A_HEADS * A_HEAD_DIM
    x = _proj_ln([a_out, b_out], [w_out[:na], w_out[na:]], x, p['ev_ln1_g'][j], p['ev_ln1_b'][j])
    return _ffn_ln(x, p['ev_ffn_w_gate'], p['ev_ffn_w_up'], p['ev_ffn_w_down'], j,
                   p['ev_ln2_g'][j], p['ev_ln2_b'][j])


def _odd_layer(x, p, j, layer, batch, seq):
    lambda_init = 0.8 - 0.6 * math.exp(-0.3 * layer)
    q, k, vt = _qkv(x, p['od_w_qkv'][j], batch, seq)
    lam4 = jnp.stack([p['od_lam_q1'][j], p['od_lam_k1'][j], p['od_lam_q2'][j], p['od_lam_k2'][j]])
    o = _diff_attn(q, k, vt, lam4, p['od_subln_g'][j], lambda_init, batch, seq)
    x, idx, gate = _proj_ln([o], [p['od_w_out'][j]], x, p['od_ln1_g'][j], p['od_ln1_b'][j],
                            router=(p['od_router_w'][j], p['od_router_b'][j]))
    return _moe(x, idx, gate, p['od_moe_w_gate'], p['od_moe_w_up'], p['od_moe_w_down'], j,
                p['od_ln2_g'][j], p['od_ln2_b'][j])


def kernel(x, ev_w_in, ev_a_kv_norm_g, ev_a_w_uk, ev_a_w_uv, ev_a_kidx_ln_g, ev_a_kidx_ln_b,
           ev_b_w_g2, ev_b_b_g, ev_b_norm_g, ev_w_out, ev_ln1_g, ev_ln1_b,
           ev_ffn_w_gate, ev_ffn_w_up, ev_ffn_w_down, ev_ln2_g, ev_ln2_b,
           od_w_qkv, od_lam_q1, od_lam_k1, od_lam_q2, od_lam_k2, od_subln_g, od_w_out,
           od_ln1_g, od_ln1_b, od_router_w, od_router_b,
           od_moe_w_gate, od_moe_w_up, od_moe_w_down, od_ln2_g, od_ln2_b):
    p = dict(locals())
    batch, seq, d = x.shape
    h = x.reshape(batch * seq, d)
    for layer in range(DEPTH):
        j = layer // 2
        if layer % 2 == 0:
            h = _even_layer(h, p, j, batch, seq)
        else:
            h = _odd_layer(h, p, j, layer, batch, seq)
    return h.reshape(batch, seq, d)
```
